```python
import functools
import jax
import jax.numpy as jnp
from jax import lax
import numpy as np

D_MODEL = 1024
BATCH = 2
SEQ = 16384
DEPTH = 1
DEC_BATCH = 128
DEC_SEQ = 1
PAST_LEN = 8192
PAGE_SIZE = 128

A_HEADS = 8
A_HEAD_DIM = 64
A_WIDTH = A_HEADS * A_HEAD_DIM
MOBA_BLOCK = 256
MOBA_TOPK = 3
Q_BLOCK = 128
B_HEADS = 4
B_HEAD_DIM = 128
B_WIDTH = B_HEADS * B_HEAD_DIM
CONV_W = 4
CONV_CH = 3 * B_WIDTH
GDN_CHUNK = 64
FF_HIDDEN = ((-(-8 * D_MODEL // 3) + 255) // 256) * 256
N_MOD = 6
EPS = 1e-6
IN_SPLITS = (A_WIDTH, A_WIDTH, A_WIDTH, CONV_CH, B_WIDTH, B_HEADS, B_HEADS, D_MODEL, D_MODEL)
IN_COLS = sum(IN_SPLITS)

kernel_name = "moba_gdn_parallel_hybrid_step"


def _rmsnorm(x, g):
    xf = x.astype(jnp.float32)
    y = xf * lax.rsqrt(jnp.mean(xf * xf, axis=-1, keepdims=True) + EPS)
    return (y * g.astype(jnp.float32)).astype(x.dtype)


def _l2norm(x):
    xf = x.astype(jnp.float32)
    return xf * lax.rsqrt(jnp.sum(xf * xf, axis=-1, keepdims=True) + EPS)


def _alibi_slopes():
    h = jnp.arange(1, A_HEADS + 1, dtype=jnp.float32)
    return jnp.exp2(-8.0 * h / A_HEADS)


def _split_cols(z):
    idx = np.cumsum(IN_SPLITS)[:-1].tolist()
    return jnp.split(z, idx, axis=-1)


def _causal_conv_silu(x_new, buf, w):
    t = x_new.shape[1]
    xp = jnp.concatenate([buf.astype(x_new.dtype), x_new], axis=1)
    y = xp[:, 0:t] * w[0]
    for j in range(1, CONV_W):
        y = y + xp[:, j:j + t] * w[j]
    return jax.nn.silu(y), xp[:, t:]


def _gated_delta_chunked(q, k, v, beta, g, s0):
    n, t, h, dk = q.shape
    dv = v.shape[-1]
    nc = -(-t // GDN_CHUNK)
    pad = nc * GDN_CHUNK - t

    def blocks(u):
        u = jnp.pad(u, ((0, 0), (0, pad)) + ((0, 0),) * (u.ndim - 2))
        u = u.reshape((n, nc, GDN_CHUNK) + u.shape[2:])
        return jnp.moveaxis(u, 3, 1)

    qc, kc, vc, bc, gc = (blocks(u) for u in (q, k, v, beta, g))
    gcum = jnp.cumsum(gc, axis=-1)
    idx = jnp.arange(GDN_CHUNK)
    incl = idx[:, None] >= idx[None, :]
    strict = idx[:, None] > idx[None, :]
    diff = gcum[..., :, None] - gcum[..., None, :]
    decay = jnp.where(incl, jnp.exp(jnp.where(incl, diff, 0.0)), 0.0)
    kb = kc * bc[..., None]
    a_mat = jnp.where(strict, jnp.einsum('nhcid,nhcjd->nhcij', kb, kc) * decay, 0.0)
    rhs = jnp.concatenate([vc * bc[..., None], kb * jnp.exp(gcum)[..., None]], axis=-1)
    sol = lax.linalg.triangular_solve(a_mat, rhs, left_side=True, lower=True, unit_diagonal=True)
    u, w = sol[..., :dv], sol[..., dv:]
    qk = jnp.einsum('nhcid,nhcjd->nhcij', qc, kc) * decay
    q_dec = qc * jnp.exp(gcum)[..., None]
    k_dec = kc * jnp.exp(gcum[..., -1:] - gcum)[..., None]
    g_last = jnp.exp(gcum[..., -1])

    def step(s, inp):
        u_c, w_c, qk_c, qd_c, kd_c, gl_c = inp
        v_new = u_c - jnp.einsum('nhcd,nhde->nhce', w_c, s)
        o = jnp.einsum('nhcd,nhde->nhce', qd_c, s) + jnp.einsum('nhij,nhje->nhie', qk_c, v_new)
        s = s * gl_c[..., None, None] + jnp.einsum('nhcd,nhce->nhde', kd_c, v_new)
        return s, o

    xs = tuple(jnp.moveaxis(a, 2, 0) for a in (u, w, qk, q_dec, k_dec, g_last))
    s_fin, o = lax.scan(step, s0, xs)
    o = jnp.transpose(o, (1, 0, 3, 2, 4)).reshape(n, nc * GDN_CHUNK, h, dv)[:, :t]
    return o, s_fin


def _moba_combine(s_own, v_own, s_sel, v_sel, spec_own):
    m = jnp.maximum(jnp.max(s_own, axis=-1), jnp.max(s_sel, axis=-1))[..., None]
    p_own = jnp.exp(s_own - m)
    p_sel = jnp.exp(s_sel - m)
    den = jnp.sum(p_own, axis=-1) + jnp.sum(p_sel, axis=-1)
    num = (jnp.einsum(spec_own, p_own, v_own.astype(jnp.float32))
           + jnp.einsum('bhqk,bhqkd->bhqd', p_sel, v_sel.astype(jnp.float32)))
    return num / den[..., None]


def _moba_prompt(q, k, v, slopes):
    n, s, h, dh = q.shape
    nb = -(-s // MOBA_BLOCK)
    pad = nb * MOBA_BLOCK - s
    topk = min(MOBA_TOPK, nb)
    scale = dh ** -0.5

    def to_blocks(u):
        u = jnp.pad(u, ((0, 0), (0, pad), (0, 0), (0, 0)))
        return u.reshape(n, nb, MOBA_BLOCK, h, dh).transpose(0, 3, 1, 2, 4)

    kblk, vblk = to_blocks(k), to_blocks(v)
    kmean = jnp.mean(kblk, axis=3, dtype=jnp.float32)
    qh = q.transpose(0, 2, 1, 3)
    offs = jnp.arange(MOBA_BLOCK)
    ni = jnp.arange(n)[:, None, None, None]
    hi = jnp.arange(h)[None, :, None, None]

    def one_query_block(c):
        start = c * Q_BLOCK
        qc = lax.dynamic_slice_in_dim(qh, start, Q_BLOCK, axis=2).astype(jnp.float32)
        qpos = start + jnp.arange(Q_BLOCK)
        own = start // MOBA_BLOCK
        k_own = lax.dynamic_index_in_dim(kblk, own, axis=2, keepdims=False)
        v_own = lax.dynamic_index_in_dim(vblk, own, axis=2, keepdims=False)
        dist_own = (qpos[:, None] - (own * MOBA_BLOCK + offs)[None, :]).astype(jnp.float32)
        s_own = jnp.einsum('bhqd,bhkd->bhqk', qc, k_own.astype(jnp.float32)) * scale - slopes[:, None, None] * dist_own
        s_own = jnp.where(dist_own >= 0, s_own, -jnp.inf)
        bscore = jnp.einsum('bhqd,bhnd->bhqn', qc, kmean)
        bscore = jnp.where(jnp.arange(nb) < own, bscore, -jnp.inf)
        _, sel = lax.top_k(bscore, topk)
        k_sel = kblk[ni, hi, sel].reshape(n, h, Q_BLOCK, topk * MOBA_BLOCK, dh)
        v_sel = vblk[ni, hi, sel].reshape(n, h, Q_BLOCK, topk * MOBA_BLOCK, dh)
        pos_sel = (sel[..., None] * MOBA_BLOCK + offs).reshape(n, h, Q_BLOCK, topk * MOBA_BLOCK)
        valid_sel = jnp.repeat(jnp.arange(topk) < own, MOBA_BLOCK)
        s_sel = (jnp.einsum('bhqd,bhqkd->bhqk', qc, k_sel.astype(jnp.float32)) * scale
                 - slopes[:, None, None] * (qpos[:, None] - pos_sel).astype(jnp.float32))
        s_sel = jnp.where(valid_sel, s_sel, -jnp.inf)
        return _moba_combine(s_own, v_own, s_sel, v_sel, 'bhqk,bhkd->bhqd').astype(q.dtype)

    out = lax.map(one_query_block, jnp.arange(s // Q_BLOCK))
    return out.transpose(1, 0, 3, 2, 4).reshape(n, s, h * dh)


def _gather_rows(cache, page_table, new, pos):
    n, h = pos.shape[0], pos.shape[1]
    n_pages = page_table.shape[1]
    t_new = new.shape[1]
    ni = jnp.arange(n)[:, None, None, None]
    hi = jnp.arange(h)[None, :, None, None]
    phys = page_table[ni, jnp.clip(pos // PAGE_SIZE, 0, n_pages - 1)]
    old = cache[phys, pos % PAGE_SIZE, hi]
    fresh = new[ni, jnp.clip(pos - PAST_LEN, 0, t_new - 1), hi]
    return jnp.where((pos >= PAST_LEN)[..., None], fresh, old)


def _moba_sample(q, k, v, cache_k, cache_v, page_table, slopes):
    n, t, h, dh = q.shape
    n_pages = page_table.shape[1]
    nb = -(-(PAST_LEN + t) // MOBA_BLOCK)
    topk = min(MOBA_TOPK, nb)
    scale = dh ** -0.5
    pool_sum = jnp.sum(cache_k, axis=1, dtype=jnp.float32)
    page_sum = pool_sum[page_table]
    oh_page = jax.nn.one_hot(jnp.arange(n_pages) * PAGE_SIZE // MOBA_BLOCK, nb, dtype=jnp.float32)
    qpos = PAST_LEN + jnp.arange(t)
    own = qpos // MOBA_BLOCK
    oh_new = jax.nn.one_hot(own, nb, dtype=jnp.float32)
    kmean = (jnp.einsum('nphd,pb->nhbd', page_sum, oh_page)
             + jnp.einsum('nthd,tb->nhbd', k.astype(jnp.float32), oh_new)) / MOBA_BLOCK
    qf = q.transpose(0, 2, 1, 3).astype(jnp.float32)
    bscore = jnp.einsum('nhtd,nhbd->nhtb', qf, kmean)
    bscore = jnp.where(jnp.arange(nb)[None, :] < own[:, None], bscore, -jnp.inf)
    _, sel = lax.top_k(bscore, topk)
    offs = jnp.arange(MOBA_BLOCK)
    pos_sel = (sel[..., None] * MOBA_BLOCK + offs).reshape(n, h, t, topk * MOBA_BLOCK)
    valid_sel = jnp.repeat(jnp.arange(topk)[None, :] < own[:, None], MOBA_BLOCK, axis=-1)
    pos_own = jnp.broadcast_to(own[:, None] * MOBA_BLOCK + offs, (n, h, t, MOBA_BLOCK))
    valid_own = pos_own <= qpos[:, None]
    k_sel = _gather_rows(cache_k, page_table, k, pos_sel)
    v_sel = _gather_rows(cache_v, page_table, v, pos_sel)
    k_own = _gather_rows(cache_k, page_table, k, pos_own)
    v_own = _gather_rows(cache_v, page_table, v, pos_own)
    s_sel = (jnp.einsum('nhtd,nhtkd->nhtk', qf, k_sel.astype(jnp.float32)) * scale
             - slopes[:, None, None] * (qpos[:, None] - pos_sel).astype(jnp.float32))
    s_sel = jnp.where(valid_sel, s_sel, -jnp.inf)
    s_own = (jnp.einsum('nhtd,nhtkd->nhtk', qf, k_own.astype(jnp.float32)) * scale
             - slopes[:, None, None] * (qpos[:, None] - pos_own).astype(jnp.float32))
    s_own = jnp.where(valid_own, s_own, -jnp.inf)
    out = _moba_combine(s_own, v_own, s_sel, v_sel, 'bhqk,bhqkd->bhqd')
    return out.transpose(0, 2, 1, 3).reshape(n, t, h * dh).astype(q.dtype)


def _layer(x, c, moba_fn, conv_buf, s0, w_ada, b_ada, g_pre_mix, g_post_mix, g_pre_ffn, g_post_ffn,
           w_in, conv_w, a_log, dt_bias, gdn_norm_g, w_branch_a, w_branch_b, w_out, w_ffn_in, w_ffn_out):
    n, t, _ = x.shape
    mod = jax.nn.silu(c) @ w_ada + b_ada
    sh1, sc1, gt1, sh2, sc2, gt2 = jnp.split(mod[:, None, :], N_MOD, axis=-1)
    h = _rmsnorm(x, g_pre_mix) * (1 + sc1) + sh1
    q_a, k_a, v_a, qkv_b, z_b, beta_raw, a_raw, ga_raw, gb_raw = _split_cols(h @ w_in)
    q_a = q_a.reshape(n, t, A_HEADS, A_HEAD_DIM)
    k_a = k_a.reshape(n, t, A_HEADS, A_HEAD_DIM)
    v_a = v_a.reshape(n, t, A_HEADS, A_HEAD_DIM)
    o_a = moba_fn(q_a, k_a, v_a)
    qkv_b, new_buf = _causal_conv_silu(qkv_b, conv_buf, conv_w)
    q_b, k_b, v_b = jnp.split(qkv_b, 3, axis=-1)
    q_b = _l2norm(q_b.reshape(n, t, B_HEADS, B_HEAD_DIM)) * (B_HEAD_DIM ** -0.5)
    k_b = _l2norm(k_b.reshape(n, t, B_HEADS, B_HEAD_DIM))
    v_b = v_b.reshape(n, t, B_HEADS, B_HEAD_DIM).astype(jnp.float32)
    beta = jax.nn.sigmoid(beta_raw.astype(jnp.float32))
    g = -jnp.exp(a_log.astype(jnp.float32)) * jax.nn.softplus(a_raw.astype(jnp.float32) + dt_bias.astype(jnp.float32))
    o_b, s_new = _gated_delta_chunked(q_b, k_b, v_b, beta, g, s0.astype(jnp.float32))
    zf = z_b.reshape(n, t, B_HEADS, B_HEAD_DIM).astype(jnp.float32)
    o_b = o_b * lax.rsqrt(jnp.mean(o_b * o_b, axis=-1, keepdims=True) + EPS) * gdn_norm_g.astype(jnp.float32) * jax.nn.silu(zf)
    o_b = o_b.reshape(n, t, B_WIDTH).astype(x.dtype)
    y = jax.nn.sigmoid(ga_raw) * (o_a @ w_branch_a) + jax.nn.sigmoid(gb_raw) * (o_b @ w_branch_b)
    x = x + gt1 * _rmsnorm(y @ w_out, g_post_mix)
    h2 = _rmsnorm(x, g_pre_ffn) * (1 + sc2) + sh2
    gate, up = jnp.split(h2 @ w_ffn_in, 2, axis=-1)
    f = (jax.nn.silu(gate) * up) @ w_ffn_out
    x = x + gt2 * _rmsnorm(f, g_post_ffn)
    return x, k_a, v_a, s_new.astype(x.dtype), new_buf


def setup_inputs(seed: int = 0) -> dict:
    key = jax.random.key(seed)
    ks = jax.random.split(key, 32)
    f32 = jnp.float32
    n_pages = PAST_LEN // PAGE_SIZE
    n_used = DEC_BATCH * n_pages
    n_pool = n_used + max(1, n_used // 4)

    def nrm(k, shape, s=1.0):
        return s * jax.random.normal(k, shape, f32)

    page_table = jax.random.permutation(ks[7], n_pool)[:n_used].reshape(DEC_BATCH, n_pages).astype(jnp.int32)
    dt = jnp.exp(jax.random.uniform(ks[18], (DEPTH, B_HEADS), f32, jnp.log(0.001), jnp.log(0.1)))
    return {
        "x_prompt": nrm(ks[0], (BATCH, SEQ, D_MODEL)),
        "x_sample": nrm(ks[1], (DEC_BATCH, DEC_SEQ, D_MODEL)),
        "cache_k": nrm(ks[2], (DEPTH, n_pool, PAGE_SIZE, A_HEADS, A_HEAD_DIM)),
        "cache_v": nrm(ks[3], (DEPTH, n_pool, PAGE_SIZE, A_HEADS, A_HEAD_DIM)),
        "state_gdn": nrm(ks[4], (DEPTH, DEC_BATCH, B_HEADS, B_HEAD_DIM, B_HEAD_DIM), 0.1),
        "state_conv": nrm(ks[5], (DEPTH, DEC_BATCH, CONV_W - 1, CONV_CH)),
        "page_table": page_table,
        "c_prompt": nrm(ks[6], (BATCH, D_MODEL)),
        "c_sample": nrm(ks[8], (DEC_BATCH, D_MODEL)),
        "w_ada": nrm(ks[9], (DEPTH, D_MODEL, N_MOD * D_MODEL), 0.5 * D_MODEL ** -0.5),
        "b_ada": nrm(ks[10], (DEPTH, N_MOD * D_MODEL), 0.02),
        "g_pre_mix": 1.0 + nrm(ks[11], (DEPTH, D_MODEL), 0.1),
        "g_post_mix": 1.0 + nrm(ks[12], (DEPTH, D_MODEL), 0.1),
        "g_pre_ffn": 1.0 + nrm(ks[13], (DEPTH, D_MODEL), 0.1),
        "g_post_ffn": 1.0 + nrm(ks[14], (DEPTH, D_MODEL), 0.1),
        "w_in": nrm(ks[15], (DEPTH, D_MODEL, IN_COLS), D_MODEL ** -0.5),
        "conv_w": nrm(ks[16], (DEPTH, CONV_W, CONV_CH), CONV_W ** -0.5),
        "a_log": jnp.log(jax.random.uniform(ks[17], (DEPTH, B_HEADS), f32, 1.0, 16.0)),
        "dt_bias": dt + jnp.log(-jnp.expm1(-dt)),
        "gdn_norm_g": 1.0 + nrm(ks[19], (DEPTH, B_HEAD_DIM), 0.1),
        "w_branch_a": nrm(ks[20], (DEPTH, A_WIDTH, D_MODEL), A_WIDTH ** -0.5),
        "w_branch_b": nrm(ks[21], (DEPTH, B_WIDTH, D_MODEL), B_WIDTH ** -0.5),
        "w_out": nrm(ks[22], (DEPTH, D_MODEL, D_MODEL), D_MODEL ** -0.5),
        "w_ffn_in": nrm(ks[23], (DEPTH, D_MODEL, 2 * FF_HIDDEN), D_MODEL ** -0.5),
        "w_ffn_out": nrm(ks[24], (DEPTH, FF_HIDDEN, D_MODEL), FF_HIDDEN ** -0.5),
    }


def reference(x_prompt, x_sample, cache_k, cache_v, state_gdn, state_conv, page_table, c_prompt, c_sample,
              w_ada, b_ada, g_pre_mix, g_post_mix, g_pre_ffn, g_post_ffn, w_in, conv_w, a_log, dt_bias,
              gdn_norm_g, w_branch_a, w_branch_b, w_out, w_ffn_in, w_ffn_out):
    slopes = _alibi_slopes()
    hp, hs = x_prompt, x_sample
    kp_l, vp_l, ks_l, vs_l, sp_l, ss_l, bp_l, bs_l = [], [], [], [], [], [], [], []
    for l in range(DEPTH):
        params = (w_ada[l], b_ada[l], g_pre_mix[l], g_post_mix[l], g_pre_ffn[l], g_post_ffn[l], w_in[l],
                  conv_w[l], a_log[l], dt_bias[l], gdn_norm_g[l], w_branch_a[l], w_branch_b[l], w_out[l],
                  w_ffn_in[l], w_ffn_out[l])
        buf0 = jnp.zeros((x_prompt.shape[0], CONV_W - 1, CONV_CH), x_prompt.dtype)
        s_zero = jnp.zeros((x_prompt.shape[0], B_HEADS, B_HEAD_DIM, B_HEAD_DIM), jnp.float32)
        hp, kp, vp, sp, bp = _layer(hp, c_prompt, functools.partial(_moba_prompt, slopes=slopes),
                                    buf0, s_zero, *params)
        moba_s = functools.partial(_moba_sample, cache_k=cache_k[l], cache_v=cache_v[l],
                                   page_table=page_table, slopes=slopes)
        hs, ks, vs, ss, bs = _layer(hs, c_sample, moba_s, state_conv[l], state_gdn[l], *params)
        kp_l.append(kp); vp_l.append(vp); ks_l.append(ks); vs_l.append(vs)
        sp_l.append(sp); ss_l.append(ss); bp_l.append(bp); bs_l.append(bs)
    y_prompt, y_sample = hp, hs
    k_prompt, v_prompt = jnp.stack(kp_l), jnp.stack(vp_l)
    k_sample, v_sample = jnp.stack(ks_l), jnp.stack(vs_l)
    gdn_prompt, gdn_sample = jnp.stack(sp_l), jnp.stack(ss_l)
    conv_prompt, conv_sample = jnp.stack(bp_l), jnp.stack(bs_l)
    return (y_prompt, y_sample, k_prompt, v_prompt, k_sample, v_sample, gdn_prompt, gdn_sample, conv_prompt, conv_sample)
```

```python
import functools

import jax
import jax.numpy as jnp
from jax import lax
from jax.experimental import pallas as pl
from jax.experimental.pallas import tpu as pltpu

f32 = jnp.float32
bf16 = jnp.bfloat16

EPS = 1e-6
MOBA_BLOCK = 256
MOBA_TOPK = 3
Q_BLOCK = 128
GDN_CHUNK = 64
N_MOD = 6

LANES = 128
SUBLANES = 8
VMEM_LIMIT_BYTES = 56 * 1024 * 1024

HIGHEST = lax.Precision.HIGHEST
NEG_INF = float("-inf")
POS_INF = float("inf")
_TRANS_B = (((1,), (1,)), ((), ()))


def _params(semantics):
    return pltpu.CompilerParams(dimension_semantics=semantics, vmem_limit_bytes=VMEM_LIMIT_BYTES)


def _dot(a, b):
    return jnp.dot(a, b, preferred_element_type=f32)


def _dot_hi(a, b):
    return jnp.dot(a, b, precision=HIGHEST, preferred_element_type=f32)


def _silu(x):
    return x * jax.nn.sigmoid(x)


def _softplus(x):
    return jnp.maximum(x, 0.0) + jnp.log1p(jnp.exp(-jnp.abs(x)))


def _rms(x):
    return x * lax.rsqrt(jnp.mean(x * x, axis=-1, keepdims=True) + EPS)


def _l2(x):
    return x * lax.rsqrt(jnp.sum(x * x, axis=-1, keepdims=True) + EPS)


def _ada_kernel(c_ref, w_ref, b_ref, o_ref):
    a = _silu(c_ref[...]).astype(bf16)
    o_ref[...] = _dot(a, w_ref[...].astype(bf16)) + b_ref[...]


def _ada(c_all, w_ada, b_ada):
    rows, d = c_all.shape
    nm = w_ada.shape[1]
    tn = 512
    return pl.pallas_call(
        _ada_kernel,
        grid=(nm // tn,),
        in_specs=[
            pl.BlockSpec((rows, d), lambda j: (0, 0)),
            pl.BlockSpec((d, tn), lambda j: (0, j)),
            pl.BlockSpec((1, tn), lambda j: (0, j)),
        ],
        out_specs=pl.BlockSpec((rows, tn), lambda j: (0, j)),
        out_shape=jax.ShapeDtypeStruct((rows, nm), f32),
        compiler_params=_params(("arbitrary",)),
        name="ada",
    )(c_all, w_ada, b_ada.reshape(1, nm))


def _prenorm(x, g, sc, sh):
    return (_rms(x) * g) * (1.0 + sc) + sh


def _proj_prompt_kernel(x_ref, sc_ref, sh_ref, g_ref, w_ref,
                        k_ref, v_ref, qa_ref, ka_ref, vt_ref, km_ref, cb_ref, zb_ref, ga_ref, gb_ref, ba_ref,
                        *, heads, hd, cols):
    tm = x_ref.shape[0]
    hb = _prenorm(x_ref[...], g_ref[...], sc_ref[...], sh_ref[...]).astype(bf16)

    def proj(name):
        a, b = cols[name]
        return _dot(hb, w_ref[:, a:b])

    q, k, v = proj("q"), proj("k"), proj("v")
    k_ref[...] = k
    v_ref[...] = v
    km_ref[0] = jnp.mean(k, axis=0, keepdims=True)
    lane = lax.broadcasted_iota(jnp.int32, (tm, LANES), 1)
    tpos = lax.broadcasted_iota(jnp.int32, (tm, LANES), 0).astype(f32)
    per = LANES // hd
    for p in range(heads // per):
        qp = q[:, p * LANES:(p + 1) * LANES] * (hd ** -0.5)
        kp = k[:, p * LANES:(p + 1) * LANES]
        for e in range(per):
            hh = p * per + e
            slope = 2.0 ** (-8.0 * (hh + 1) / heads)
            qe = qp if e == 0 else pltpu.roll(qp, LANES - e * hd, axis=1)
            ke = kp if e == 0 else pltpu.roll(kp, LANES - e * hd, axis=1)
            qa_ref[hh] = jnp.where(lane < hd, qe, jnp.where(lane == hd, 1.0, 0.0)).astype(bf16)
            ka_ref[hh] = jnp.where(lane < hd, ke, jnp.where(lane == hd, slope * tpos, 0.0)).astype(bf16)
        vt_ref[p * LANES:(p + 1) * LANES, :] = v[:, p * LANES:(p + 1) * LANES].T.astype(bf16)
    cb_ref[...] = proj("cb")
    zb_ref[...] = proj("zb")
    ga_ref[...] = proj("ga")
    gb_ref[...] = proj("gb")
    ba_ref[...] = proj("ba")


def _proj_sample_kernel(x_ref, sc_ref, sh_ref, g_ref, w_ref,
                        q_ref, k_ref, v_ref, cb_ref, zb_ref, ga_ref, gb_ref, ba_ref, *, cols):
    hb = _prenorm(x_ref[...], g_ref[...], sc_ref[...], sh_ref[...]).astype(bf16)
    for name, ref in (("q", q_ref), ("k", k_ref), ("v", v_ref), ("cb", cb_ref), ("zb", zb_ref),
                      ("ga", ga_ref), ("gb", gb_ref), ("ba", ba_ref)):
        a, b = cols[name]
        ref[...] = _dot(hb, w_ref[:, a:b])


def _layout_w_in(w_in, a_width, conv_ch, b_width, b_heads, d):
    o_q, o_k, o_v = 0, a_width, 2 * a_width
    o_cb = 3 * a_width
    o_zb = o_cb + conv_ch
    o_ba = o_zb + b_width
    o_ga = o_ba + 2 * b_heads
    o_gb = o_ga + d
    main = w_in[:, :o_ba]
    gates = w_in[:, o_ga:o_gb + d]
    ba = w_in[:, o_ba:o_ga]
    pad = jnp.zeros((w_in.shape[0], LANES - 2 * b_heads), w_in.dtype)
    w = jnp.concatenate([main, gates, ba, pad], axis=1).astype(bf16)
    cols = {"q": (o_q, o_k), "k": (o_k, o_v), "v": (o_v, o_cb), "cb": (o_cb, o_zb), "zb": (o_zb, o_ba),
            "ga": (o_ba, o_ba + d), "gb": (o_ba + d, o_ba + 2 * d), "ba": (o_ba + 2 * d, o_ba + 2 * d + LANES)}
    return w, cols


def _proj_prompt(x, sc, sh, g, w, cols, heads, hd, conv_ch, b_width):
    n, t, d = x.shape
    tm = MOBA_BLOCK
    nb = t // tm
    aw = heads * hd
    wcols = w.shape[1]
    row = lambda n_, i: (n_ * nb + i, 0)
    mod = lambda n_, i: (n_, 0, 0)
    out_shape = (
        jax.ShapeDtypeStruct((n * t, aw), f32),
        jax.ShapeDtypeStruct((n * t, aw), f32),
        jax.ShapeDtypeStruct((n, heads, t, LANES), bf16),
        jax.ShapeDtypeStruct((n, heads, t, LANES), bf16),
        jax.ShapeDtypeStruct((n, nb, aw, tm), bf16),
        jax.ShapeDtypeStruct((n * nb, 1, aw), f32),
        jax.ShapeDtypeStruct((n * t, conv_ch), f32),
        jax.ShapeDtypeStruct((n * t, b_width), f32),
        jax.ShapeDtypeStruct((n * t, d), f32),
        jax.ShapeDtypeStruct((n * t, d), f32),
        jax.ShapeDtypeStruct((n * t, LANES), f32),
    )
    out_specs = (
        pl.BlockSpec((tm, aw), row),
        pl.BlockSpec((tm, aw), row),
        pl.BlockSpec((None, heads, tm, LANES), lambda n_, i: (n_, 0, i, 0)),
        pl.BlockSpec((None, heads, tm, LANES), lambda n_, i: (n_, 0, i, 0)),
        pl.BlockSpec((None, None, aw, tm), lambda n_, i: (n_, i, 0, 0)),
        pl.BlockSpec((1, 1, aw), lambda n_, i: (n_ * nb + i, 0, 0)),
        pl.BlockSpec((tm, conv_ch), row),
        pl.BlockSpec((tm, b_width), row),
        pl.BlockSpec((tm, d), row),
        pl.BlockSpec((tm, d), row),
        pl.BlockSpec((tm, LANES), row),
    )
    return pl.pallas_call(
        functools.partial(_proj_prompt_kernel, heads=heads, hd=hd, cols=cols),
        grid=(n, nb),
        in_specs=[
            pl.BlockSpec((tm, d), row),
            pl.BlockSpec((None, 1, d), mod),
            pl.BlockSpec((None, 1, d), mod),
            pl.BlockSpec((1, d), lambda n_, i: (0, 0)),
            pl.BlockSpec((d, wcols), lambda n_, i: (0, 0)),
        ],
        out_specs=out_specs,
        out_shape=out_shape,
        compiler_params=_params(("arbitrary", "arbitrary")),
        name="proj_prompt",
    )(x.reshape(n * t, d), sc, sh, g, w)


def _proj_sample(x, sc, sh, g, w, cols):
    rows, d = x.shape
    wcols = w.shape[1]
    names = ("q", "k", "v", "cb", "zb", "ga", "gb", "ba")
    widths = [cols[nm][1] - cols[nm][0] for nm in names]
    full = lambda i: (0, 0)
    return pl.pallas_call(
        functools.partial(_proj_sample_kernel, cols=cols),
        grid=(1,),
        in_specs=[
            pl.BlockSpec((rows, d), full),
            pl.BlockSpec((rows, d), full),
            pl.BlockSpec((rows, d), full),
            pl.BlockSpec((1, d), full),
            pl.BlockSpec((d, wcols), full),
        ],
        out_specs=tuple(pl.BlockSpec((rows, wd), full) for wd in widths),
        out_shape=tuple(jax.ShapeDtypeStruct((rows, wd), f32) for wd in widths),
        compiler_params=_params(("arbitrary",)),
        name="proj_sample",
    )(x, sc, sh, g, w)


def _moba_prompt_kernel(q_ref, k_ref, vt_ref, kmh_ref, kml_ref, o_ref, sel_ref, *, heads, topk):
    h = pl.program_id(1)
    c = pl.program_id(2)
    qb = q_ref.shape[0]
    nbp = kmh_ref.shape[0]
    mb = k_ref.shape[1]
    own = (c * qb) // mb
    q = q_ref[...]

    bs = (lax.dot_general(kmh_ref[...], q, _TRANS_B, preferred_element_type=f32)
          + lax.dot_general(kml_ref[...], q, _TRANS_B, preferred_element_type=f32))
    bid = lax.broadcasted_iota(jnp.int32, (nbp, qb), 0)
    bs = jnp.where(bid < own, bs, NEG_INF)
    sel = jnp.zeros((nbp, qb), f32)
    for t in range(topk):
        mx = jnp.max(bs, axis=0, keepdims=True)
        idx = jnp.min(jnp.where(bs == mx, bid, nbp), axis=0, keepdims=True)
        hit = bid == idx
        sel = jnp.where(jnp.logical_and(hit, t < own), 1.0, sel)
        bs = jnp.where(hit, NEG_INF, bs)
    sel_ref[...] = sel

    s = lax.dot_general(k_ref[own], q, _TRANS_B, preferred_element_type=f32)
    kr = lax.broadcasted_iota(jnp.int32, (mb, qb), 0)
    ql = lax.broadcasted_iota(jnp.int32, (mb, qb), 1)
    s = jnp.where(kr <= ql + (c * qb - own * mb), s, NEG_INF)
    m0 = jnp.max(s, axis=0, keepdims=True)
    p = jnp.exp(s - m0)
    l0 = jnp.sum(p, axis=0, keepdims=True)
    acc0 = _dot(vt_ref[own], p.astype(bf16))
    block_bias = lax.shift_left(1, (heads - 1) - h)

    def body(j, carry):
        m, l, acc = carry
        on = sel_ref[pl.ds(j, 1), :] > 0.0
        s = lax.dot_general(k_ref[j], q, _TRANS_B, preferred_element_type=f32)
        off = jnp.full((1, qb), (j - own) * block_bias, jnp.int32).astype(f32)
        m_new = jnp.where(on, jnp.maximum(m, jnp.max(s, axis=0, keepdims=True) + off), m)
        alpha = jnp.exp(m - m_new)
        p = jnp.exp(s - jnp.where(on, m_new - off, POS_INF))
        l = alpha * l + jnp.sum(p, axis=0, keepdims=True)
        acc = alpha * acc + _dot(vt_ref[j], p.astype(bf16))
        return m_new, l, acc

    _, l, acc = lax.fori_loop(0, own, body, (m0, l0, acc0))
    o_ref[...] = (acc / l).astype(o_ref.dtype)


def _moba_prompt(qa, ka, vt, km, heads, hd):
    n, _, t, _ = qa.shape
    nb = t // MOBA_BLOCK
    nbp = km.shape[2]
    nq = t // Q_BLOCK
    km_hi = km.astype(bf16)
    km_lo = (km - km_hi.astype(f32)).astype(bf16)
    ka = ka.reshape(n, heads, nb, MOBA_BLOCK, LANES)
    whole = lambda n_, h, c: (n_, h, 0, 0)
    return pl.pallas_call(
        functools.partial(_moba_prompt_kernel, heads=heads, topk=MOBA_TOPK),
        grid=(n, heads, nq),
        in_specs=[
            pl.BlockSpec((None, None, Q_BLOCK, LANES), lambda n_, h, c: (n_, h, c, 0)),
            pl.BlockSpec((None, None, nb, MOBA_BLOCK, LANES), lambda n_, h, c: (n_, h, 0, 0, 0)),
            pl.BlockSpec((None, nb, hd, MOBA_BLOCK), lambda n_, h, c: (n_, 0, h, 0)),
            pl.BlockSpec((None, None, nbp, LANES), whole),
            pl.BlockSpec((None, None, nbp, LANES), whole),
        ],
        out_specs=pl.BlockSpec((None, hd, Q_BLOCK), lambda n_, h, c: (n_, h, c)),
        out_shape=jax.ShapeDtypeStruct((n, heads * hd, t), bf16),
        scratch_shapes=[pltpu.VMEM((nbp, Q_BLOCK), f32)],
        compiler_params=_params(("arbitrary", "arbitrary", "arbitrary")),
        name="moba_prompt",
    )(qa, ka, vt, km_hi, km_lo)


def _page_sum_kernel(c_ref, o_ref):
    o_ref[...] = jnp.sum(c_ref[...], axis=1)


def _page_sums(cache, layer):
    _, n_pool, page, heads, hd = cache.shape
    pb = 8
    while n_pool % pb:
        pb //= 2
    return pl.pallas_call(
        _page_sum_kernel,
        grid=(n_pool // pb,),
        in_specs=[pl.BlockSpec((None, pb, page, heads, hd), lambda i: (layer, i, 0, 0, 0))],
        out_specs=pl.BlockSpec((pb, heads, hd), lambda i: (i, 0, 0)),
        out_shape=jax.ShapeDtypeStruct((n_pool, heads, hd), f32),
        compiler_params=_params(("arbitrary",)),
        name="page_sums",
    )(cache)


def _sel_sample_kernel(pt_ref, q_ref, kn_ref, ps_ref, sel_ref, buf_ref, sem, *, n_pages, ppb, topk, own):
    i = pl.program_id(0)

    def page_copy(p, src_page):
        return pltpu.make_async_copy(ps_ref.at[src_page], buf_ref.at[p], sem)

    for p in range(n_pages):
        page_copy(p, pt_ref[i * n_pages + p]).start()
    for p in range(n_pages):
        page_copy(p, 0).wait()
    heads, hd = q_ref.shape
    nbk = n_pages // ppb
    blk = MOBA_BLOCK
    ksum = jnp.sum(buf_ref[...].reshape(nbk, ppb, heads, hd), axis=1)
    q = q_ref[...]
    sc_past = jnp.sum(ksum * q[None], axis=-1, keepdims=True) / blk
    sc_new = jnp.sum(kn_ref[...] * q, axis=-1, keepdims=True)[None] / blk
    bs = jnp.concatenate([sc_past, sc_new], axis=0)
    bid = lax.broadcasted_iota(jnp.int32, bs.shape, 0)
    bs = jnp.where(bid < own, bs, NEG_INF)
    for t in range(topk):
        mx = jnp.max(bs, axis=0, keepdims=True)
        idx = jnp.min(jnp.where(bs == mx, bid, nbk + 1), axis=0, keepdims=True)
        sel_ref[t] = jnp.broadcast_to(idx[0], (heads, LANES))
        bs = jnp.where(bid == idx, NEG_INF, bs)


def _attn_sample_kernel(pt_ref, sel_ref, q_ref, kn_ref, vn_ref, sl_ref, k0_ref, k1_ref, v0_ref, v1_ref,
                        o_ref, m_ref, l_ref, acc_ref, *, past, own, page, topk):
    i = pl.program_id(0)
    h = pl.program_id(1)
    t = pl.program_id(2)
    heads, hd = q_ref.shape
    scale = hd ** -0.5
    sub = lax.broadcasted_iota(jnp.int32, (heads, hd), 0)
    mine = sub == h
    q = q_ref[...]
    slope = sl_ref[pl.ds(h, 1), :][:, 0:1]

    @pl.when(t == 0)
    def _():
        s_own = jnp.sum(jnp.sum(jnp.where(mine, q * kn_ref[...], 0.0), axis=-1, keepdims=True),
                        axis=0, keepdims=True) * scale
        m_ref[...] = jnp.broadcast_to(s_own, m_ref.shape)
        l_ref[...] = jnp.ones_like(l_ref)
        acc_ref[...] = jnp.sum(jnp.where(mine, vn_ref[...], 0.0), axis=0, keepdims=True)

    qm = jnp.where(mine, q * scale, 0.0).astype(bf16)
    rows = page * heads
    col = lax.broadcasted_iota(jnp.int32, (1, rows), 1)
    blk = sel_ref[(i * heads + h) * topk + t]
    ok = jnp.logical_and((col & (heads - 1)) == h, t < own)
    for j, (k_ref, v_ref) in enumerate(((k0_ref, v0_ref), (k1_ref, v1_ref))):
        k2 = k_ref[...].reshape(rows, hd).astype(bf16)
        v2 = v_ref[...].reshape(rows, hd).astype(bf16)
        s = jnp.sum(lax.dot_general(qm, k2, _TRANS_B, preferred_element_type=f32), axis=0, keepdims=True)
        pos = blk * MOBA_BLOCK + j * page + (col >> 3)
        s = jnp.where(ok, s - slope * (past - pos).astype(f32), NEG_INF)
        m = m_ref[:, 0:1]
        m_new = jnp.maximum(m, jnp.max(s, axis=-1, keepdims=True))
        alpha = jnp.exp(m - m_new)
        p = jnp.exp(s - m_new)
        l_ref[...] = alpha * l_ref[...] + jnp.sum(p, axis=-1, keepdims=True)
        pv = _dot(jnp.broadcast_to(p, (SUBLANES, rows)).astype(bf16), v2)
        acc_ref[...] = alpha * acc_ref[...] + pv[0:1]
        m_ref[...] = jnp.broadcast_to(m_new, m_ref.shape)

    @pl.when(t == topk - 1)
    def _():
        o_ref[pl.ds(h, 1), :] = acc_ref[...] / l_ref[:, 0:1]


def _moba_sample(q_s, k_s, v_s, cache_k, cache_v, page_table, layer):
    dec = q_s.shape[0]
    _, n_pool, page, heads, hd = cache_k.shape
    n_pages = page_table.shape[1]
    past = n_pages * page
    ppb = MOBA_BLOCK // page
    assert MOBA_BLOCK % page == 0 and past % MOBA_BLOCK == 0 and ppb == 2 and heads == SUBLANES
    own = past // MOBA_BLOCK
    topk = min(MOBA_TOPK, own + 1)
    q3 = q_s.reshape(dec, heads, hd)
    kn3 = k_s.reshape(dec, heads, hd)
    vn3 = v_s.reshape(dec, heads, hd)
    psum = _page_sums(cache_k, layer)
    pt_flat = page_table.reshape(-1).astype(jnp.int32)
    per_sample = lambda i, *_: (i, 0, 0)

    sel = pl.pallas_call(
        functools.partial(_sel_sample_kernel, n_pages=n_pages, ppb=ppb, topk=topk, own=own),
        grid_spec=pltpu.PrefetchScalarGridSpec(
            num_scalar_prefetch=1,
            grid=(dec,),
            in_specs=[
                pl.BlockSpec((None, heads, hd), per_sample),
                pl.BlockSpec((None, heads, hd), per_sample),
                pl.BlockSpec(memory_space=pl.ANY),
            ],
            out_specs=pl.BlockSpec((None, topk, heads, LANES), lambda i, *_: (i, 0, 0, 0)),
            scratch_shapes=[pltpu.VMEM((n_pages, heads, hd), f32), pltpu.SemaphoreType.DMA],
        ),
        out_shape=jax.ShapeDtypeStruct((dec, topk, heads, LANES), jnp.int32),
        compiler_params=_params(("arbitrary",)),
        name="moba_sample_select",
    )(pt_flat, q3, kn3, psum)
    sel_flat = jnp.transpose(sel[..., 0], (0, 2, 1)).reshape(-1)

    slopes = jnp.broadcast_to(jnp.exp2(-8.0 * jnp.arange(1, heads + 1, dtype=f32) / heads)[:, None], (heads, LANES))

    def page_map(j):
        def index(i, h, t, pt, sl):
            return (layer, pt[i * n_pages + ppb * sl[(i * heads + h) * topk + t] + j], 0, 0, 0)
        return index

    page_spec = lambda j: pl.BlockSpec((None, None, page, heads, hd), page_map(j))
    small = lambda i, h, t, *_: (i, 0, 0)
    o3 = pl.pallas_call(
        functools.partial(_attn_sample_kernel, past=past, own=own, page=page, topk=topk),
        grid_spec=pltpu.PrefetchScalarGridSpec(
            num_scalar_prefetch=2,
            grid=(dec, heads, topk),
            in_specs=[
                pl.BlockSpec((None, heads, hd), small),
                pl.BlockSpec((None, heads, hd), small),
                pl.BlockSpec((None, heads, hd), small),
                pl.BlockSpec((heads, LANES), lambda i, h, t, *_: (0, 0)),
                page_spec(0), page_spec(1),
                page_spec(0), page_spec(1),
            ],
            out_specs=pl.BlockSpec((None, heads, hd), small),
            scratch_shapes=[pltpu.VMEM((1, LANES), f32), pltpu.VMEM((1, LANES), f32), pltpu.VMEM((1, hd), f32)],
        ),
        out_shape=jax.ShapeDtypeStruct((dec, heads, hd), f32),
        compiler_params=_params(("arbitrary", "arbitrary", "arbitrary")),
        name="moba_sample_attend",
    )(pt_flat, sel_flat, q3, kn3, vn3, slopes, cache_k, cache_k, cache_v, cache_v)
    return o3.reshape(dec, heads * hd)


def _gdn_masks(ch):
    ri = lax.broadcasted_iota(jnp.int32, (ch, ch), 0)
    ci = lax.broadcasted_iota(jnp.int32, (ch, ch), 1)
    levels = []
    b = 1
    while b < ch:
        levels.append(jnp.logical_and(jnp.logical_and(ri // (2 * b) == ci // (2 * b), ri % (2 * b) >= b),
                                      ci % (2 * b) < b))
        b *= 2
    return ri, ci, levels


def _unit_lower_inverse(a, eye, levels):
    t = eye - jnp.where(levels[0], a, 0.0)
    for lvl in levels[1:]:
        r = jnp.where(lvl, a, 0.0)
        t = t - _dot_hi(_dot_hi(t, r), t)
    return t


def _gdn_prompt_kernel(cb_ref, ba_ref, zb_ref, cw_ref, pa_ref, pb_ref, gn_ref, o_ref, sfin_ref, xp_ref, s_ref,
                       *, heads, dk, ch, taps):
    i = pl.program_id(1)
    td = cb_ref.shape[0]
    bw = heads * dk
    pad = SUBLANES

    @pl.when(i == 0)
    def _():
        xp_ref[0:pad, :] = jnp.zeros((pad, xp_ref.shape[1]), f32)
        s_ref[...] = jnp.zeros_like(s_ref)

    xp_ref[pad:pad + td, :] = cb_ref[...]
    w = cw_ref[...]
    y = w[0:1] * xp_ref[pad - (taps - 1):pad - (taps - 1) + td, :]
    for j in range(1, taps):
        y = y + w[j:j + 1] * xp_ref[pad - (taps - 1) + j:pad - (taps - 1) + j + td, :]
    xp_ref[0:pad, :] = xp_ref[td:td + pad, :]
    c = _silu(y)

    ba = ba_ref[...]
    beta_all = jax.nn.sigmoid(ba)
    g_all = -jnp.exp(pa_ref[...]) * _softplus(ba + pb_ref[...])
    ri, ci, levels = _gdn_masks(ch)
    incl = ri >= ci
    strict = ri > ci
    eye = (ri == ci).astype(f32)
    tri = incl.astype(f32)
    ones = jnp.ones((ch, ch), f32)
    gn = gn_ref[...]

    for h in range(heads):
        qn = _l2(c[:, h * dk:(h + 1) * dk]) * (dk ** -0.5)
        kn = _l2(c[:, bw + h * dk:bw + (h + 1) * dk])
        vh = c[:, 2 * bw + h * dk:2 * bw + (h + 1) * dk]
        beta_b = jnp.broadcast_to(beta_all[:, h:h + 1], (td, dk))
        g_b = jnp.broadcast_to(g_all[:, heads + h:heads + h + 1], (td, dk))
        zh = zb_ref[:, h * dk:(h + 1) * dk]
        s = s_ref[h]
        for cc in range(td // ch):
            sl = slice(cc * ch, (cc + 1) * ch)
            q_c, k_c, v_c, b_c = qn[sl], kn[sl], vh[sl], beta_b[sl]
            gcum = _dot_hi(tri, g_b[sl])
            gcol = gcum[:, :ch]
            grow = _dot_hi(ones, jnp.where(ri == ci, gcol, 0.0))
            decay = jnp.where(incl, jnp.exp(jnp.where(incl, gcol - grow, 0.0)), 0.0)
            eg = jnp.exp(gcum)
            kb = k_c * b_c
            kk = lax.dot_general(kb.astype(bf16), k_c.astype(bf16), _TRANS_B, preferred_element_type=f32)
            tinv = _unit_lower_inverse(jnp.where(strict, kk * decay, 0.0), eye, levels)
            u = _dot_hi(tinv, v_c * b_c)
            wv = _dot_hi(tinv, kb * eg)
            qk = lax.dot_general(q_c.astype(bf16), k_c.astype(bf16), _TRANS_B, preferred_element_type=f32) * decay
            g_end = gcum[ch - 1:ch, :]
            k_dec = k_c * jnp.exp(g_end - gcum)
            sb = s.astype(bf16)
            v_new = u - _dot(wv.astype(bf16), sb)
            o = _dot((q_c * eg).astype(bf16), sb) + _dot(qk.astype(bf16), v_new.astype(bf16))
            s = s * jnp.exp(g_end) + lax.dot_general(k_dec.astype(bf16), v_new.astype(bf16),
                                                     (((0,), (0,)), ((), ())), preferred_element_type=f32)
            on = _rms(o) * gn * _silu(zh[sl])
            o_ref[sl, h * dk:(h + 1) * dk] = on.astype(o_ref.dtype)
        s_ref[h] = s
        sfin_ref[h] = s


def _gdn_prompt(cb, ba, zb, conv_w, pa, pb, gn, n, t, heads, dk):
    td = 256
    conv_ch = cb.shape[1]
    bw = heads * dk
    nt = t // td
    row = lambda n_, i: (n_ * nt + i, 0)
    const = lambda n_, i: (0, 0)
    return pl.pallas_call(
        functools.partial(_gdn_prompt_kernel, heads=heads, dk=dk, ch=GDN_CHUNK, taps=conv_w.shape[0]),
        grid=(n, nt),
        in_specs=[
            pl.BlockSpec((td, conv_ch), row),
            pl.BlockSpec((td, LANES), row),
            pl.BlockSpec((td, bw), row),
            pl.BlockSpec(conv_w.shape, const),
            pl.BlockSpec((1, LANES), const),
            pl.BlockSpec((1, LANES), const),
            pl.BlockSpec((1, dk), const),
        ],
        out_specs=(
            pl.BlockSpec((td, bw), row),
            pl.BlockSpec((None, heads, dk, dk), lambda n_, i: (n_, 0, 0, 0)),
        ),
        out_shape=(
            jax.ShapeDtypeStruct((n * t, bw), bf16),
            jax.ShapeDtypeStruct((n, heads, dk, dk), f32),
        ),
        scratch_shapes=[pltpu.VMEM((td + SUBLANES, conv_ch), f32), pltpu.VMEM((heads, dk, dk), f32)],
        compiler_params=_params(("arbitrary", "arbitrary")),
        name="gdn_prompt",
    )(cb, ba, zb, conv_w, pa, pb, gn)


def _gdn_sample_kernel(cb_ref, b0_ref, b1_ref, b2_ref, ba_ref, zb_ref, s_ref, cw_ref, pa_ref, pb_ref, gn_ref,
                       o_ref, so_ref, *, heads, dk):
    sb = cb_ref.shape[0]
    bw = heads * dk
    w = cw_ref[...]
    y = w[0:1] * b0_ref[...]
    y = y + w[1:2] * b1_ref[...]
    y = y + w[2:3] * b2_ref[...]
    y = y + w[3:4] * cb_ref[...]
    c = _silu(y)
    ba = ba_ref[...]
    beta_all = jax.nn.sigmoid(ba)
    g_all = -jnp.exp(pa_ref[...]) * _softplus(ba + pb_ref[...])
    ri = lax.broadcasted_iota(jnp.int32, (dk, dk), 0)
    ci = lax.broadcasted_iota(jnp.int32, (dk, dk), 1)
    ones = jnp.ones((dk, dk), f32)
    gn = gn_ref[...]
    zrows = jnp.zeros((SUBLANES - 2, dk), f32)
    for h in range(heads):
        qn = _l2(c[:, h * dk:(h + 1) * dk]) * (dk ** -0.5)
        kn = _l2(c[:, bw + h * dk:bw + (h + 1) * dk])
        vh = c[:, 2 * bw + h * dk:2 * bw + (h + 1) * dk]
        beta_b = jnp.broadcast_to(beta_all[:, h:h + 1], (sb, dk))
        eg = jnp.exp(jnp.broadcast_to(g_all[:, heads + h:heads + h + 1], (sb, dk)))
        zh = zb_ref[:, h * dk:(h + 1) * dk]
        qk = jnp.sum(qn * kn, axis=-1, keepdims=True)
        for j in range(sb):
            r = slice(j, j + 1)
            s = s_ref[j, h]
            lhs = jnp.concatenate([kn[r] * beta_b[r] * eg[r], qn[r] * eg[r], zrows], axis=0)
            rs = _dot_hi(lhs, s)
            v_new = vh[r] * beta_b[r] - rs[0:1]
            o = rs[1:2] + qk[r] * v_new
            kcol = _dot_hi(jnp.where(ri == ci, jnp.broadcast_to(kn[r], (dk, dk)), 0.0), ones)
            so_ref[j, h] = s * eg[r] + kcol * v_new
            on = _rms(o) * gn * _silu(zh[r])
            o_ref[r, h * dk:(h + 1) * dk] = on.astype(o_ref.dtype)


def _gdn_sample(cb, bufs, ba, zb, state, conv_w, pa, pb, gn, heads, dk):
    dec, conv_ch = cb.shape
    bw = heads * dk
    sb = SUBLANES
    row = lambda i: (i, 0)
    const = lambda i: (0, 0)
    st = lambda i: (i, 0, 0, 0)
    return pl.pallas_call(
        functools.partial(_gdn_sample_kernel, heads=heads, dk=dk),
        grid=(dec // sb,),
        in_specs=[
            pl.BlockSpec((sb, conv_ch), row),
            pl.BlockSpec((sb, conv_ch), row),
            pl.BlockSpec((sb, conv_ch), row),
            pl.BlockSpec((sb, conv_ch), row),
            pl.BlockSpec((sb, LANES), row),
            pl.BlockSpec((sb, bw), row),
            pl.BlockSpec((sb, heads, dk, dk), st),
            pl.BlockSpec(conv_w.shape, const),
            pl.BlockSpec((1, LANES), const),
            pl.BlockSpec((1, LANES), const),
            pl.BlockSpec((1, dk), const),
        ],
        out_specs=(pl.BlockSpec((sb, bw), row), pl.BlockSpec((sb, heads, dk, dk), st)),
        out_shape=(jax.ShapeDtypeStruct((dec, bw), bf16), jax.ShapeDtypeStruct(state.shape, f32)),
        compiler_params=_params(("arbitrary",)),
        name="gdn_sample",
    )(cb, bufs[0], bufs[1], bufs[2], ba, zb, state, conv_w, pa, pb, gn)


def _merge_kernel(oa_ref, ob_ref, ga_ref, gb_ref, x_ref, gt_ref, gpm_ref, wa_ref, wb_ref, wo_ref, o_ref,
                  *, oa_transposed):
    if oa_transposed:
        ya = lax.dot_general(oa_ref[...], wa_ref[...], (((0,), (0,)), ((), ())), preferred_element_type=f32)
    else:
        ya = _dot(oa_ref[...].astype(bf16), wa_ref[...])
    yb = _dot(ob_ref[...], wb_ref[...])
    y = jax.nn.sigmoid(ga_ref[...]) * ya + jax.nn.sigmoid(gb_ref[...]) * yb
    y2 = _dot(y.astype(bf16), wo_ref[...])
    o_ref[...] = x_ref[...] + gt_ref[...] * (_rms(y2) * gpm_ref[...])


def _merge(oa, ob, ga, gb, x2, gt, gpm, wa, wb, wo, n, t, tm, oa_transposed):
    d = x2.shape[1]
    aw = wa.shape[0]
    bw = wb.shape[0]
    nt = t // tm
    row = lambda n_, i: (n_ * nt + i, 0)
    const = lambda n_, i: (0, 0)
    if oa_transposed:
        oa_spec = pl.BlockSpec((None, aw, tm), lambda n_, i: (n_, 0, i))
    else:
        oa_spec = pl.BlockSpec((tm, aw), row)
    return pl.pallas_call(
        functools.partial(_merge_kernel, oa_transposed=oa_transposed),
        grid=(n, nt),
        in_specs=[
            oa_spec,
            pl.BlockSpec((tm, bw), row),
            pl.BlockSpec((tm, d), row),
            pl.BlockSpec((tm, d), row),
            pl.BlockSpec((tm, d), row),
            pl.BlockSpec((None, gt.shape[1], d), lambda n_, i: (n_, 0, 0)),
            pl.BlockSpec((1, d), const),
            pl.BlockSpec(wa.shape, const),
            pl.BlockSpec(wb.shape, const),
            pl.BlockSpec(wo.shape, const),
        ],
        out_specs=pl.BlockSpec((tm, d), row),
        out_shape=jax.ShapeDtypeStruct(x2.shape, f32),
        compiler_params=_params(("arbitrary", "arbitrary")),
        name="merge",
    )(oa, ob, ga, gb, x2, gt, gpm, wa, wb, wo)


def _ffn_kernel(x_ref, sc_ref, sh_ref, gt_ref, gpre_ref, gpost_ref, wi_ref, wo_ref, o_ref, *, ff, fc):
    x1 = x_ref[...]
    hb = _prenorm(x1, gpre_ref[...], sc_ref[...], sh_ref[...]).astype(bf16)
    f = jnp.zeros(x1.shape, f32)
    for c in range(ff // fc):
        gate = _dot(hb, wi_ref[:, c * fc:(c + 1) * fc])
        up = _dot(hb, wi_ref[:, ff + c * fc:ff + (c + 1) * fc])
        f = f + _dot((_silu(gate) * up).astype(bf16), wo_ref[c * fc:(c + 1) * fc, :])
    o_ref[...] = x1 + gt_ref[...] * (_rms(f) * gpost_ref[...])


def _ffn(x2, sc, sh, gt, gpre, gpost, wi, wo, n, t, tm):
    d = x2.shape[1]
    ff = wo.shape[0]
    fc = ff // 2 if (ff // 2) % LANES == 0 else ff
    nt = t // tm
    row = lambda n_, i: (n_ * nt + i, 0)
    const = lambda n_, i: (0, 0)
    mod = pl.BlockSpec((None, sc.shape[1], d), lambda n_, i: (n_, 0, 0))
    return pl.pallas_call(
        functools.partial(_ffn_kernel, ff=ff, fc=fc),
        grid=(n, nt),
        in_specs=[
            pl.BlockSpec((tm, d), row), mod, mod, mod,
            pl.BlockSpec((1, d), const),
            pl.BlockSpec((1, d), const),
            pl.BlockSpec(wi.shape, const),
            pl.BlockSpec(wo.shape, const),
        ],
        out_specs=pl.BlockSpec((tm, d), row),
        out_shape=jax.ShapeDtypeStruct(x2.shape, f32),
        compiler_params=_params(("arbitrary", "arbitrary")),
        name="ffn",
    )(x2, sc, sh, gt, gpre, gpost, wi, wo)


def kernel(x_prompt, x_sample, cache_k, cache_v, state_gdn, state_conv, page_table, c_prompt, c_sample,
           w_ada, b_ada, g_pre_mix, g_post_mix, g_pre_ffn, g_post_ffn, w_in, conv_w, a_log, dt_bias,
           gdn_norm_g, w_branch_a, w_branch_b, w_out, w_ffn_in, w_ffn_out):
    depth = w_in.shape[0]
    n, t, d = x_prompt.shape
    dec, dec_t, _ = x_sample.shape
    heads, hd = cache_k.shape[3], cache_k.shape[4]
    b_heads, dk = state_gdn.shape[2], state_gdn.shape[3]
    aw, bw = heads * hd, b_heads * dk
    conv_ch = conv_w.shape[2]
    taps = conv_w.shape[1]
    assert dec_t == 1 and heads == SUBLANES and 2 * hd == LANES and dk == LANES
    assert t % MOBA_BLOCK == 0 and MOBA_BLOCK % Q_BLOCK == 0 and dec % SUBLANES == 0 and taps == 4

    xp, xs = x_prompt, x_sample.reshape(dec, d)
    outs = {k: [] for k in ("kp", "vp", "ks", "vs", "sp", "ss", "bp", "bs")}
    rows = -(-(n + dec) // SUBLANES) * SUBLANES
    c_all = jnp.concatenate([c_prompt, c_sample, jnp.zeros((rows - n - dec, d), f32)], axis=0)
    nb = t // MOBA_BLOCK
    nbp = max(nb, SUBLANES)

    for l in range(depth):
        mod = _ada(c_all, w_ada[l], b_ada[l])
        mods = [mod[:, j * d:(j + 1) * d] for j in range(N_MOD)]
        mp = [m[:n].reshape(n, 1, d) for m in mods]
        ms = [m[n:n + dec].reshape(1, dec, d) for m in mods]
        w, cols = _layout_w_in(w_in[l], aw, conv_ch, bw, b_heads, d)
        row = lambda a: a.reshape(1, -1)
        g1, g2, g3, g4 = row(g_pre_mix[l]), row(g_post_mix[l]), row(g_pre_ffn[l]), row(g_post_ffn[l])
        wa, wb, wo = w_branch_a[l].astype(bf16), w_branch_b[l].astype(bf16), w_out[l].astype(bf16)
        wi, wf = w_ffn_in[l].astype(bf16), w_ffn_out[l].astype(bf16)
        lane_pad = lambda a: jnp.zeros((1, LANES), f32).at[0, b_heads:2 * b_heads].set(a)
        pa, pb = lane_pad(a_log[l]), lane_pad(dt_bias[l])
        gn = row(gdn_norm_g[l])

        k2, v2, qa, ka, vt, km, cb, zb, ga, gb, ba = _proj_prompt(xp, mp[1], mp[0], g1, w, cols, heads, hd,
                                                                   conv_ch, bw)
        km = km.reshape(n, nb, heads, hd).transpose(0, 2, 1, 3)
        km = jnp.pad(km, ((0, 0), (0, 0), (0, nbp - nb), (0, LANES - hd)))
        oa_t = _moba_prompt(qa, ka, vt, km, heads, hd)
        ob, s_fin = _gdn_prompt(cb, ba, zb, conv_w[l], pa, pb, gn, n, t, b_heads, dk)
        x2 = xp.reshape(n * t, d)
        x1 = _merge(oa_t, ob, ga, gb, x2, mp[2], g2, wa, wb, wo, n, t, MOBA_BLOCK, True)
        xp = _ffn(x1, mp[4], mp[3], mp[5], g3, g4, wi, wf, n, t, MOBA_BLOCK).reshape(n, t, d)
        outs["kp"].append(k2.reshape(n, t, heads, hd))
        outs["vp"].append(v2.reshape(n, t, heads, hd))
        outs["sp"].append(s_fin)
        outs["bp"].append(cb.reshape(n, t, conv_ch)[:, t - (taps - 1):])

        q_s, k_s, v_s, cb_s, zb_s, ga_s, gb_s, ba_s = _proj_sample(xs, ms[1][0], ms[0][0], g1, w, cols)
        oa_s = _moba_sample(q_s, k_s, v_s, cache_k, cache_v, page_table, l)
        bufs = [state_conv[l][:, j] for j in range(taps - 1)]
        ob_s, s_new = _gdn_sample(cb_s, bufs, ba_s, zb_s, state_gdn[l], conv_w[l], pa, pb, gn, b_heads, dk)
        x1s = _merge(oa_s, ob_s, ga_s, gb_s, xs, ms[2], g2, wa, wb, wo, 1, dec, dec, False)
        xs = _ffn(x1s, ms[4], ms[3], ms[5], g3, g4, wi, wf, 1, dec, dec)
        outs["ks"].append(k_s.reshape(dec, 1, heads, hd))
        outs["vs"].append(v_s.reshape(dec, 1, heads, hd))
        outs["ss"].append(s_new)
        outs["bs"].append(jnp.stack([bufs[1], bufs[2], cb_s], axis=1))

    st = lambda key: jnp.stack(outs[key])
    return (xp, xs.reshape(dec, 1, d), st("kp"), st("vp"), st("ks"), st("vs"), st("sp"), st("ss"), st("bp"), st("bs"))
```

```python
import functools

import jax
import jax.numpy as jnp
from jax import lax
from jax.experimental import pallas as pl
from jax.experimental.pallas import tpu as pltpu

f32 = jnp.float32
bf16 = jnp.bfloat16

EPS = 1e-6
MOBA_BLOCK = 256
MOBA_TOPK = 3
Q_BLOCK = 128
GDN_CHUNK = 64
N_MOD = 6

LANES = 128
SUBLANES = 8
V_PAD_ROWS = 16
MOBA_DEPTH = 4
VMEM_LIMIT_BYTES = 56 * 1024 * 1024

HIGHEST = lax.Precision.HIGHEST
NEG_INF = float("-inf")
POS_INF = float("inf")
LOG2E = 1.4426950408889634
_TRANS_B = (((1,), (1,)), ((), ()))
_TRANS_A = (((0,), (0,)), ((), ()))


def _params(semantics):
    return pltpu.CompilerParams(dimension_semantics=semantics, vmem_limit_bytes=VMEM_LIMIT_BYTES)


def _dot(a, b):
    return jnp.dot(a, b, preferred_element_type=f32)


def _dot_nt(a, b):
    return lax.dot_general(a, b, _TRANS_B, preferred_element_type=f32)


def _dot_hi(a, b):
    return jnp.dot(a, b, precision=HIGHEST, preferred_element_type=f32)


def _split(x):
    hi = x.astype(bf16)
    return hi, (x - hi.astype(f32)).astype(bf16)


def _dot3(a, b):
    ah, al = _split(a)
    bh, bl = _split(b)
    return _dot(ah, bh) + (_dot(ah, bl) + _dot(al, bh))


def _dot_mask3(mask, x):
    x1 = x.astype(bf16)
    r1 = x - x1.astype(f32)
    x2 = r1.astype(bf16)
    x3 = (r1 - x2.astype(f32)).astype(bf16)
    return _dot(mask, x1) + (_dot(mask, x2) + _dot(mask, x3))


def _silu(x):
    return x * jax.nn.sigmoid(x)


def _softplus(x):
    return jnp.maximum(x, 0.0) + jnp.log1p(jnp.exp(-jnp.abs(x)))


def _rms(x):
    return x * lax.rsqrt(jnp.mean(x * x, axis=-1, keepdims=True) + EPS)


def _l2(x):
    return x * lax.rsqrt(jnp.sum(x * x, axis=-1, keepdims=True) + EPS)


def _alibi_slope(h, heads):
    return 2.0 ** (-8.0 * (h + 1) / heads)


def _ada_kernel(c_ref, w_ref, b_ref, o_ref):
    a = _silu(c_ref[...]).astype(bf16)
    o_ref[...] = _dot(a, w_ref[...].astype(bf16)) + b_ref[...]


def _ada(c_all, w_ada, b_ada):
    rows, d = c_all.shape
    nm = w_ada.shape[1]
    tn = 512
    return pl.pallas_call(
        _ada_kernel,
        grid=(nm // tn,),
        in_specs=[
            pl.BlockSpec((rows, d), lambda j: (0, 0)),
            pl.BlockSpec((d, tn), lambda j: (0, j)),
            pl.BlockSpec((1, tn), lambda j: (0, j)),
        ],
        out_specs=pl.BlockSpec((rows, tn), lambda j: (0, j)),
        out_shape=jax.ShapeDtypeStruct((rows, nm), f32),
        compiler_params=_params(("arbitrary",)),
        name="ada",
    )(c_all, w_ada, b_ada.reshape(1, nm))


def _prenorm(x, g, sc, sh):
    return (_rms(x) * g) * (1.0 + sc) + sh


def _proj_prompt_kernel(x_ref, sc_ref, sh_ref, g_ref, w_ref,
                        kt_ref, vt_ref, q16_ref, k16_ref, vt16_ref, km_ref, cb_ref, zb_ref, ga_ref, gb_ref, ba_ref,
                        *, hd, cols):
    hb = _prenorm(x_ref[...], g_ref[...], sc_ref[...], sh_ref[...]).astype(bf16)

    def proj(name):
        a, b = cols[name]
        return _dot(hb, w_ref[:, a:b])

    q, k, v = proj("q"), proj("k"), proj("v")
    tm = x_ref.shape[0]
    q16_ref[...] = (q * (hd ** -0.5 * LOG2E)).astype(bf16)
    km_ref[0] = jnp.mean(k, axis=0, keepdims=True)
    vrows = vt16_ref.shape[1]
    ones_row = (lax.broadcasted_iota(jnp.int32, (vrows - hd, tm), 0) == 0).astype(bf16)
    for g in range(k.shape[1] // LANES):
        cs = slice(g * LANES, (g + 1) * LANES)
        k16_ref[g] = k[:, cs].astype(bf16)
        kt_ref[cs, :] = k[:, cs].T
        vt = v[:, cs].T
        vt_ref[cs, :] = vt
        for e in range(LANES // hd):
            h = g * (LANES // hd) + e
            vt16_ref[h, 0:hd, :] = vt[e * hd:(e + 1) * hd].astype(bf16)
            vt16_ref[h, hd:vrows, :] = ones_row
    cb_ref[...] = proj("cb")
    zb_ref[...] = proj("zb")
    ga_ref[...] = proj("ga")
    gb_ref[...] = proj("gb")
    ba_ref[...] = proj("ba")


def _proj_sample_kernel(x_ref, sc_ref, sh_ref, g_ref, w_ref,
                        q_ref, k_ref, v_ref, cb_ref, zb_ref, ga_ref, gb_ref, ba_ref, *, cols):
    hb = _prenorm(x_ref[...], g_ref[...], sc_ref[...], sh_ref[...]).astype(bf16)
    for name, ref in (("q", q_ref), ("k", k_ref), ("v", v_ref), ("cb", cb_ref), ("zb", zb_ref),
                      ("ga", ga_ref), ("gb", gb_ref), ("ba", ba_ref)):
        a, b = cols[name]
        ref[...] = _dot(hb, w_ref[:, a:b])


def _layout_w_in(w_in, a_width, conv_ch, b_width, b_heads, d):
    o_q, o_k, o_v = 0, a_width, 2 * a_width
    o_cb = 3 * a_width
    o_zb = o_cb + conv_ch
    o_ba = o_zb + b_width
    o_ga = o_ba + 2 * b_heads
    o_gb = o_ga + d
    main = w_in[:, :o_ba]
    gates = w_in[:, o_ga:o_gb + d]
    ba = w_in[:, o_ba:o_ga]
    pad = jnp.zeros((w_in.shape[0], LANES - 2 * b_heads), w_in.dtype)
    w = jnp.concatenate([main, gates, ba, pad], axis=1).astype(bf16)
    cols = {"q": (o_q, o_k), "k": (o_k, o_v), "v": (o_v, o_cb), "cb": (o_cb, o_zb), "zb": (o_zb, o_ba),
            "ga": (o_ba, o_ba + d), "gb": (o_ba + d, o_ba + 2 * d), "ba": (o_ba + 2 * d, o_ba + 2 * d + LANES)}
    return w, cols


def _proj_prompt(x, sc, sh, g, w, cols, heads, hd, conv_ch, b_width):
    n, t, d = x.shape
    tm = MOBA_BLOCK
    nb = t // tm
    aw = heads * hd
    wcols = w.shape[1]
    row = lambda n_, i: (n_ * nb + i, 0)
    mod = lambda n_, i: (n_, 0, 0)
    tok_minor = lambda n_, i: (n_, 0, i)
    out_shape = (
        jax.ShapeDtypeStruct((n, aw, t), f32),
        jax.ShapeDtypeStruct((n, aw, t), f32),
        jax.ShapeDtypeStruct((n * t, aw), bf16),
        jax.ShapeDtypeStruct((n, nb, aw // LANES, tm, LANES), bf16),
        jax.ShapeDtypeStruct((n, nb, heads, hd + V_PAD_ROWS, tm), bf16),
        jax.ShapeDtypeStruct((n * nb, 1, aw), f32),
        jax.ShapeDtypeStruct((n * t, conv_ch), f32),
        jax.ShapeDtypeStruct((n * t, b_width), f32),
        jax.ShapeDtypeStruct((n * t, d), f32),
        jax.ShapeDtypeStruct((n * t, d), f32),
        jax.ShapeDtypeStruct((n * t, LANES), f32),
    )
    out_specs = (
        pl.BlockSpec((None, aw, tm), tok_minor),
        pl.BlockSpec((None, aw, tm), tok_minor),
        pl.BlockSpec((tm, aw), row),
        pl.BlockSpec((None, None, aw // LANES, tm, LANES), lambda n_, i: (n_, i, 0, 0, 0)),
        pl.BlockSpec((None, None, heads, hd + V_PAD_ROWS, tm), lambda n_, i: (n_, i, 0, 0, 0)),
        pl.BlockSpec((1, 1, aw), lambda n_, i: (n_ * nb + i, 0, 0)),
        pl.BlockSpec((tm, conv_ch), row),
        pl.BlockSpec((tm, b_width), row),
        pl.BlockSpec((tm, d), row),
        pl.BlockSpec((tm, d), row),
        pl.BlockSpec((tm, LANES), row),
    )
    return pl.pallas_call(
        functools.partial(_proj_prompt_kernel, hd=hd, cols=cols),
        grid=(n, nb),
        in_specs=[
            pl.BlockSpec((tm, d), row),
            pl.BlockSpec((None, 1, d), mod),
            pl.BlockSpec((None, 1, d), mod),
            pl.BlockSpec((1, d), lambda n_, i: (0, 0)),
            pl.BlockSpec((d, wcols), lambda n_, i: (0, 0)),
        ],
        out_specs=out_specs,
        out_shape=out_shape,
        compiler_params=_params(("arbitrary", "arbitrary")),
        name="proj_prompt",
    )(x.reshape(n * t, d), sc, sh, g, w)


def _proj_sample(x, sc, sh, g, w, cols):
    rows, d = x.shape
    wcols = w.shape[1]
    names = ("q", "k", "v", "cb", "zb", "ga", "gb", "ba")
    widths = [cols[nm][1] - cols[nm][0] for nm in names]
    full = lambda i: (0, 0)
    return pl.pallas_call(
        functools.partial(_proj_sample_kernel, cols=cols),
        grid=(1,),
        in_specs=[
            pl.BlockSpec((rows, d), full),
            pl.BlockSpec((rows, d), full),
            pl.BlockSpec((rows, d), full),
            pl.BlockSpec((1, d), full),
            pl.BlockSpec((d, wcols), full),
        ],
        out_specs=tuple(pl.BlockSpec((rows, wd), full) for wd in widths),
        out_shape=tuple(jax.ShapeDtypeStruct((rows, wd), f32) for wd in widths),
        compiler_params=_params(("arbitrary",)),
        name="proj_sample",
    )(x, sc, sh, g, w)


def _moba_prompt_kernel(q_ref, k_ref, vt_ref, kmh_ref, kml_ref, bias_ref, o_ref,
                        qq_ref, sel_ref, m_ref, acc_ref, s_ref, st_ref, *, heads, hd, topk):
    c = pl.program_id(1)
    qb = q_ref.shape[0]
    nbp = kmh_ref.shape[0]
    mb = k_ref.shape[2]
    per = LANES // hd
    pairs = heads // per
    own = (c * qb) // mb
    lane = lax.broadcasted_iota(jnp.int32, (qb, LANES), 1)
    bid = lax.broadcasted_iota(jnp.int32, (nbp, qb), 0)
    kr = lax.broadcasted_iota(jnp.int32, (mb, qb), 0)
    ql = lax.broadcasted_iota(jnp.int32, (mb, qb), 1)
    causal = kr <= ql + (c * qb - own * mb)

    for g in range(pairs):
        pair = slice(g * LANES, (g + 1) * LANES)
        qpair = q_ref[:, pair]
        for e in range(per):
            keep = jnp.logical_and(lane >= e * hd, lane < (e + 1) * hd)
            qq_ref[g, e * qb:(e + 1) * qb, :] = jnp.where(keep, qpair, jnp.zeros_like(qpair))
        qq = qq_ref[g]
        bs2 = _dot_nt(kmh_ref[:, pair], qq) + _dot_nt(kml_ref[:, pair], qq)
        s2 = _dot_nt(k_ref[own, g], qq)
        for e in range(per):
            h = g * per + e
            bs = jnp.where(bid < own, bs2[:, e * qb:(e + 1) * qb], NEG_INF)
            sel = jnp.zeros((nbp, qb), f32)
            for t in range(topk):
                mx = jnp.max(bs, axis=0, keepdims=True)
                idx = jnp.min(jnp.where(bs == mx, bid, nbp), axis=0, keepdims=True)
                hit = bid == idx
                sel = jnp.where(jnp.logical_and(hit, t < own), 1.0, sel)
                bs = jnp.where(hit, NEG_INF, bs)
            sel_ref[h] = sel
            s = jnp.where(causal, s2[:, e * qb:(e + 1) * qb] + bias_ref[h], NEG_INF)
            m0 = jnp.max(s, axis=0, keepdims=True)
            m_ref[h] = m0
            acc_ref[h] = _dot(vt_ref[own, h], jnp.exp2(s - m0).astype(bf16))

    last = k_ref.shape[0] - 1

    def scores(j, slot):
        j = jnp.minimum(j, last)
        for g in range(pairs):
            s2 = _dot_nt(k_ref[j, g], qq_ref[g])
            for e in range(per):
                h = g * per + e
                on = sel_ref[h, pl.ds(j, 1), :] > 0.0
                s = s2[:, e * qb:(e + 1) * qb] + bias_ref[h]
                s_ref[slot, h] = s
                off = jnp.full((1, qb), j - own, jnp.int32).astype(f32) * (_alibi_slope(h, heads) * mb * LOG2E)
                m = m_ref[h]
                m_new = jnp.where(on, jnp.maximum(m, jnp.max(s, axis=0, keepdims=True) + off), m)
                st_ref[slot, h] = jnp.concatenate([jnp.exp2(m - m_new), jnp.where(on, m_new - off, POS_INF)], axis=0)
                m_ref[h] = m_new

    def consume(j, slot):
        j = jnp.minimum(j, last)
        for h in range(heads):
            st = st_ref[slot, h]
            p = jnp.exp2(s_ref[slot, h] - st[1:2])
            acc_ref[h] = st[0:1] * acc_ref[h] + _dot(vt_ref[j, h], p.astype(bf16))

    depth = s_ref.shape[0] // 2
    for u in range(depth):
        scores(u, u)

    def body(i, carry):
        base = depth * lax.rem(i, 2)
        for u in range(depth):
            consume(depth * i + u, base + u)
        for u in range(depth):
            scores(depth * (i + 1) + u, depth - base + u)
        return carry

    lax.fori_loop(0, (own + depth - 1) // depth, body, 0)
    for h in range(heads):
        acc = acc_ref[h]
        o_ref[h * hd:(h + 1) * hd, :] = (acc[:hd] / acc[hd:hd + 1]).astype(o_ref.dtype)


def _moba_prompt(q16, k16, vt16, km, n, t, heads, hd):
    aw = heads * hd
    nb = t // MOBA_BLOCK
    nq = t // Q_BLOCK
    per = LANES // hd
    pairs = heads // per
    vrows = vt16.shape[3]
    nbp = -(-nb // SUBLANES) * SUBLANES
    assert all(_alibi_slope(h, heads) * MOBA_BLOCK == int(_alibi_slope(h, heads) * MOBA_BLOCK) for h in range(heads))
    km = jnp.pad(km.reshape(n, nb, aw), ((0, 0), (0, nbp - nb), (0, 0)))
    km_hi = km.astype(bf16)
    km_lo = (km - km_hi.astype(f32)).astype(bf16)
    pos = jnp.arange(MOBA_BLOCK, dtype=f32)[None, :, None]
    slopes = jnp.asarray([_alibi_slope(h, heads) for h in range(heads)], f32)[:, None, None]
    bias = jnp.broadcast_to(slopes * pos * LOG2E, (heads, MOBA_BLOCK, Q_BLOCK))
    whole = lambda n_, c: (n_, 0, 0)
    resident = dict(pipeline_mode=pl.Buffered(1))
    return pl.pallas_call(
        functools.partial(_moba_prompt_kernel, heads=heads, hd=hd, topk=MOBA_TOPK),
        grid=(n, nq),
        in_specs=[
            pl.BlockSpec((None, Q_BLOCK, aw), lambda n_, c: (n_, c, 0)),
            pl.BlockSpec((None, nb, pairs, MOBA_BLOCK, LANES), lambda n_, c: (n_, 0, 0, 0, 0), **resident),
            pl.BlockSpec((None, nb, heads, vrows, MOBA_BLOCK), lambda n_, c: (n_, 0, 0, 0, 0), **resident),
            pl.BlockSpec((None, nbp, aw), whole),
            pl.BlockSpec((None, nbp, aw), whole),
            pl.BlockSpec((heads, MOBA_BLOCK, Q_BLOCK), lambda n_, c: (0, 0, 0)),
        ],
        out_specs=pl.BlockSpec((None, aw, Q_BLOCK), lambda n_, c: (n_, 0, c)),
        out_shape=jax.ShapeDtypeStruct((n, aw, t), bf16),
        scratch_shapes=[
            pltpu.VMEM((pairs, per * Q_BLOCK, LANES), bf16),
            pltpu.VMEM((heads, nbp, Q_BLOCK), f32),
            pltpu.VMEM((heads, 1, Q_BLOCK), f32),
            pltpu.VMEM((heads, vrows, Q_BLOCK), f32),
            pltpu.VMEM((2 * MOBA_DEPTH, heads, MOBA_BLOCK, Q_BLOCK), f32),
            pltpu.VMEM((2 * MOBA_DEPTH, heads, 2, Q_BLOCK), f32),
        ],
        compiler_params=_params(("arbitrary", "arbitrary")),
        name="moba_prompt",
    )(q16.reshape(n, t, aw), k16, vt16, km_hi, km_lo, bias)


def _sel_sample_kernel(pt_ref, q_ref, kn_ref, ck_ref, sel_ref, buf_ref, sem, *, layer, n_pages, ppb, topk, own):
    i = pl.program_id(0)
    slot = lax.rem(i, 2)

    def page_copy(sample, slot_, p):
        return pltpu.make_async_copy(ck_ref.at[layer, pt_ref[sample * n_pages + p]], buf_ref.at[slot_, p],
                                     sem.at[slot_])

    @pl.when(i == 0)
    def _():
        for p in range(n_pages):
            page_copy(0, 0, p).start()

    @pl.when(i + 1 < pl.num_programs(0))
    def _():
        for p in range(n_pages):
            page_copy(i + 1, 1 - slot, p).start()

    for p in range(n_pages):
        page_copy(i, slot, p).wait()

    heads, hd = q_ref.shape
    nbk = n_pages // ppb
    lane = lax.broadcasted_iota(jnp.int32, (hd, LANES), 1)
    q = q_ref[...]
    rows = []
    for h in range(heads):
        def add_block(b, m, h=h):
            s = buf_ref[slot, ppb * b, h]
            for j in range(1, ppb):
                s = s + buf_ref[slot, ppb * b + j, h]
            return jnp.where(lane == b, jnp.sum(s, axis=-1, keepdims=True), m)

        ksum_t = lax.fori_loop(0, nbk, add_block, jnp.zeros((hd, LANES), f32), unroll=min(8, nbk))
        rows.append(_dot_hi(q, ksum_t)[h:h + 1])
    lane8 = lax.broadcasted_iota(jnp.int32, (heads, LANES), 1)
    bs = jnp.concatenate(rows, axis=0) / MOBA_BLOCK
    sc_new = jnp.sum(kn_ref[...] * q, axis=-1, keepdims=True) / MOBA_BLOCK
    bs = jnp.where(lane8 == own, sc_new, bs)
    bs = jnp.where(lane8 < own, bs, NEG_INF)
    out = jnp.zeros((heads, LANES), jnp.int32)
    for t in range(topk):
        mx = jnp.max(bs, axis=-1, keepdims=True)
        idx = jnp.min(jnp.where(bs == mx, lane8, LANES), axis=-1, keepdims=True)
        out = jnp.where(lane8 == t, idx, out)
        bs = jnp.where(lane8 == idx, NEG_INF, bs)
    sel_ref[...] = out


def _attn_sample_kernel(pt_ref, sel_ref, q_ref, kn_ref, vn_ref, ck_ref, cv_ref, o_ref, kbuf, vbuf, sem,
                        *, layer, n_pages, ppb, topk, own, past):
    i = pl.program_id(0)
    slot = lax.rem(i, 2)
    heads, hd = q_ref.shape
    page = kbuf.shape[-1]

    def slab_copies(sample, slot_):
        out = []
        for h in range(heads):
            for t in range(topk):
                blk = sel_ref[(sample * heads + h) * topk + t]
                for j in range(ppb):
                    pg = pt_ref[sample * n_pages + ppb * blk + j]
                    r = (h * topk + t) * ppb + j
                    out.append(pltpu.make_async_copy(ck_ref.at[layer, pg, h], kbuf.at[slot_, r], sem.at[0, slot_]))
                    out.append(pltpu.make_async_copy(cv_ref.at[layer, pg, h], vbuf.at[slot_, r], sem.at[1, slot_]))
        return out

    @pl.when(i == 0)
    def _():
        for cp in slab_copies(0, 0):
            cp.start()

    @pl.when(i + 1 < pl.num_programs(0))
    def _():
        for cp in slab_copies(i + 1, 1 - slot):
            cp.start()

    for cp in slab_copies(i, slot):
        cp.wait()

    scale = hd ** -0.5
    q = q_ref[...]
    qb = (q * scale).astype(bf16)
    lane = lax.broadcasted_iota(jnp.int32, (1, page), 1)
    for h in range(heads):
        slope = _alibi_slope(h, heads)
        pieces = []
        for t in range(topk):
            blk = sel_ref[(i * heads + h) * topk + t]
            for j in range(ppb):
                r = (h * topk + t) * ppb + j
                if t < own:
                    s = _dot(qb, kbuf[slot, r].astype(bf16))[h:h + 1]
                    pos = blk * MOBA_BLOCK + j * page + lane
                    pieces.append(s - slope * (past - pos).astype(f32))
                else:
                    pieces.append(jnp.full((1, page), NEG_INF, f32))
        s_sel = jnp.concatenate(pieces, axis=1)
        s_own = jnp.sum(q[h:h + 1] * kn_ref[h:h + 1, :], axis=-1, keepdims=True) * scale
        m = jnp.maximum(jnp.max(s_sel, axis=-1, keepdims=True), s_own)
        p = jnp.exp(s_sel - m)
        p_own = jnp.exp(s_own - m)
        den = jnp.sum(p, axis=-1, keepdims=True) + p_own
        num = p_own * vn_ref[h:h + 1, :]
        for x in range(topk * ppb):
            px = jnp.broadcast_to(p[:, x * page:(x + 1) * page], (SUBLANES, page)).astype(bf16)
            num = num + _dot_nt(px, vbuf[slot, h * topk * ppb + x].astype(bf16))[0:1]
        o_ref[h:h + 1, :] = num / den


def _moba_sample(q_s, k_s, v_s, cache_k, cache_v, page_table, layer):
    dec = q_s.shape[0]
    _, n_pool, page, heads, hd = cache_k.shape
    n_pages = page_table.shape[1]
    past = n_pages * page
    ppb = MOBA_BLOCK // page
    assert MOBA_BLOCK % page == 0 and past % MOBA_BLOCK == 0 and page == LANES
    own = past // MOBA_BLOCK
    assert own < LANES
    topk = min(MOBA_TOPK, own + 1)
    q3 = q_s.reshape(dec, heads, hd)
    kn3 = k_s.reshape(dec, heads, hd)
    vn3 = v_s.reshape(dec, heads, hd)
    ck = jnp.transpose(cache_k, (0, 1, 3, 4, 2))
    cv = jnp.transpose(cache_v, (0, 1, 3, 4, 2))
    pt_flat = page_table.reshape(-1).astype(jnp.int32)
    per_sample = lambda i, *_: (i, 0, 0)
    hbm = pl.BlockSpec(memory_space=pl.ANY)

    sel = pl.pallas_call(
        functools.partial(_sel_sample_kernel, layer=layer, n_pages=n_pages, ppb=ppb, topk=topk, own=own),
        grid_spec=pltpu.PrefetchScalarGridSpec(
            num_scalar_prefetch=1,
            grid=(dec,),
            in_specs=[pl.BlockSpec((None, heads, hd), per_sample), pl.BlockSpec((None, heads, hd), per_sample), hbm],
            out_specs=pl.BlockSpec((None, heads, LANES), per_sample),
            scratch_shapes=[pltpu.VMEM((2, n_pages, heads, hd, page), f32), pltpu.SemaphoreType.DMA((2,))],
        ),
        out_shape=jax.ShapeDtypeStruct((dec, heads, LANES), jnp.int32),
        compiler_params=_params(("arbitrary",)),
        name="moba_sample_select",
    )(pt_flat, q3, kn3, ck)
    sel_flat = sel[:, :, :topk].reshape(-1)

    n_slabs = heads * topk * ppb
    o3 = pl.pallas_call(
        functools.partial(_attn_sample_kernel, layer=layer, n_pages=n_pages, ppb=ppb, topk=topk, own=own, past=past),
        grid_spec=pltpu.PrefetchScalarGridSpec(
            num_scalar_prefetch=2,
            grid=(dec,),
            in_specs=[pl.BlockSpec((None, heads, hd), per_sample), pl.BlockSpec((None, heads, hd), per_sample),
                      pl.BlockSpec((None, heads, hd), per_sample), hbm, hbm],
            out_specs=pl.BlockSpec((None, heads, hd), per_sample),
            scratch_shapes=[pltpu.VMEM((2, n_slabs, hd, page), f32), pltpu.VMEM((2, n_slabs, hd, page), f32),
                            pltpu.SemaphoreType.DMA((2, 2))],
        ),
        out_shape=jax.ShapeDtypeStruct((dec, heads, hd), f32),
        compiler_params=_params(("arbitrary",)),
        name="moba_sample_attend",
    )(pt_flat, sel_flat, q3, kn3, vn3, ck, cv)
    return o3.reshape(dec, heads * hd)


def _gdn_masks(td, ch):
    ri = lax.broadcasted_iota(jnp.int32, (td, td), 0)
    ci = lax.broadcasted_iota(jnp.int32, (td, td), 1)
    same = ri // ch == ci // ch
    masks = [jnp.logical_and(same, ri >= ci), jnp.logical_and(same, ri > ci)]
    b = 1
    while b < ch:
        masks.append(jnp.logical_and(jnp.logical_and(ri // (2 * b) == ci // (2 * b), ri % (2 * b) >= b),
                                     ci % (2 * b) < b))
        b *= 2
    return [m.astype(f32) for m in masks]


def _gdn_prompt_kernel(cb_ref, ba_ref, zb_ref, cw_ref, pa_ref, pb_ref, gn_ref, o_ref, sfin_ref,
                       xp_ref, s_ref, mk_ref, *, heads, dk, ch, taps):
    nseq, td = cb_ref.shape[0], cb_ref.shape[1]
    bw = heads * dk
    pad = SUBLANES
    nlev = mk_ref.shape[0] - 2

    @pl.when(pl.program_id(0) == 0)
    def _():
        for x, m in enumerate(_gdn_masks(td, ch)):
            mk_ref[x] = m
        xp_ref[:, 0:pad, :] = jnp.zeros((nseq, pad, xp_ref.shape[2]), f32)
        s_ref[...] = jnp.zeros_like(s_ref)

    w = cw_ref[...]
    gn = gn_ref[...]
    for b in range(nseq):
        xp_ref[b, pad:pad + td, :] = cb_ref[b]
        y = w[0:1] * xp_ref[b, pad - (taps - 1):pad - (taps - 1) + td, :]
        for j in range(1, taps):
            y = y + w[j:j + 1] * xp_ref[b, pad - (taps - 1) + j:pad - (taps - 1) + j + td, :]
        xp_ref[b, 0:pad, :] = xp_ref[b, td:td + pad, :]
        c = _silu(y)

        ba = ba_ref[b]
        beta_all = jax.nn.sigmoid(ba)
        g_all = -jnp.exp(pa_ref[...]) * _softplus(ba + pb_ref[...])

        for h in range(heads):
            incl, strict = mk_ref[0], mk_ref[1]
            qn = _l2(c[:, h * dk:(h + 1) * dk]) * (dk ** -0.5)
            kn = _l2(c[:, bw + h * dk:bw + (h + 1) * dk])
            vh = c[:, 2 * bw + h * dk:2 * bw + (h + 1) * dk]
            beta_b = jnp.broadcast_to(beta_all[:, h:h + 1], (td, dk))
            g_b = jnp.broadcast_to(g_all[:, heads + h:heads + h + 1], (td, dk))
            gcum = _dot_mask3(incl.astype(bf16), g_b)
            gcol = jnp.concatenate([gcum] * (td // dk), axis=1)
            diff = (gcol - gcol.T) * incl
            decay = jnp.exp(diff) * incl
            eg = jnp.exp(gcum)
            kb = kn * beta_b
            kb16, kn16 = kb.astype(bf16), kn.astype(bf16)
            a = _dot_nt(kb16, kn16) * decay * strict
            e = -(a * mk_ref[2])
            for lv in range(1, nlev):
                r = a * mk_ref[2 + lv]
                y = r + _dot(e.astype(bf16), r.astype(bf16))
                e = e - y - _dot(y.astype(bf16), e.astype(bf16))
            rhs = jnp.concatenate([vh * beta_b, kb * eg], axis=1)
            uw = rhs + _dot3(e, rhs)
            u, wv = uw[:, :dk], uw[:, dk:]
            qk = _dot_nt(qn.astype(bf16), kn16) * decay
            g_end = jnp.concatenate([jnp.broadcast_to(gcum[(x + 1) * ch - 1:(x + 1) * ch, :], (ch, dk))
                                     for x in range(td // ch)], axis=0)
            k_dec = (kn * jnp.exp(g_end - gcum)).astype(bf16)
            q_dec = (qn * eg).astype(bf16)
            wv16 = wv.astype(bf16)
            s = s_ref[b, h]
            v_new, o_inter = [], []
            for x in range(td // ch):
                sl = slice(x * ch, (x + 1) * ch)
                sb = s.astype(bf16)
                vn = u[sl] - _dot(wv16[sl], sb)
                o_inter.append(_dot(q_dec[sl], sb))
                s = s * jnp.exp(g_end[(x + 1) * ch - 1:(x + 1) * ch, :]) + lax.dot_general(
                    k_dec[sl], vn.astype(bf16), _TRANS_A, preferred_element_type=f32)
                v_new.append(vn)
            o = jnp.concatenate(o_inter, axis=0) + _dot(qk.astype(bf16),
                                                        jnp.concatenate(v_new, axis=0).astype(bf16))
            on = _rms(o) * gn * _silu(zb_ref[b, :, h * dk:(h + 1) * dk])
            o_ref[b, :, h * dk:(h + 1) * dk] = on.astype(o_ref.dtype)
            s_ref[b, h] = s
            sfin_ref[b, h] = s


def _gdn_prompt(cb, ba, zb, conv_w, pa, pb, gn, n, t, heads, dk):
    td = 2 * dk
    conv_ch = cb.shape[1]
    bw = heads * dk
    nt = t // td
    ch = GDN_CHUNK
    assert t % td == 0 and td % ch == 0 and ch & (ch - 1) == 0
    nmask = 2 + ch.bit_length() - 1
    tile = lambda i: (0, i, 0)
    const = lambda i: (0, 0)
    o, s_fin = pl.pallas_call(
        functools.partial(_gdn_prompt_kernel, heads=heads, dk=dk, ch=ch, taps=conv_w.shape[0]),
        grid=(nt,),
        in_specs=[
            pl.BlockSpec((n, td, conv_ch), tile),
            pl.BlockSpec((n, td, LANES), tile),
            pl.BlockSpec((n, td, bw), tile),
            pl.BlockSpec(conv_w.shape, const),
            pl.BlockSpec((1, LANES), const),
            pl.BlockSpec((1, LANES), const),
            pl.BlockSpec((1, dk), const),
        ],
        out_specs=(
            pl.BlockSpec((n, td, bw), tile),
            pl.BlockSpec((n, heads, dk, dk), lambda i: (0, 0, 0, 0)),
        ),
        out_shape=(
            jax.ShapeDtypeStruct((n, t, bw), bf16),
            jax.ShapeDtypeStruct((n, heads, dk, dk), f32),
        ),
        scratch_shapes=[pltpu.VMEM((n, td + SUBLANES, conv_ch), f32), pltpu.VMEM((n, heads, dk, dk), f32),
                        pltpu.VMEM((nmask, td, td), f32)],
        compiler_params=_params(("arbitrary",)),
        name="gdn_prompt",
    )(cb.reshape(n, t, conv_ch), ba.reshape(n, t, LANES), zb.reshape(n, t, bw), conv_w, pa, pb, gn)
    return o.reshape(n * t, bw), s_fin


def _gdn_sample_kernel(cb_ref, b0_ref, b1_ref, b2_ref, ba_ref, zb_ref, s_ref, cw_ref, pa_ref, pb_ref, gn_ref,
                       o_ref, so_ref, *, heads, dk):
    sb = cb_ref.shape[0]
    bw = heads * dk
    w = cw_ref[...]
    y = w[0:1] * b0_ref[...]
    y = y + w[1:2] * b1_ref[...]
    y = y + w[2:3] * b2_ref[...]
    y = y + w[3:4] * cb_ref[...]
    c = _silu(y)
    ba = ba_ref[...]
    beta_all = jax.nn.sigmoid(ba)
    g_all = -jnp.exp(pa_ref[...]) * _softplus(ba + pb_ref[...])
    ri = lax.broadcasted_iota(jnp.int32, (dk, dk), 0)
    ci = lax.broadcasted_iota(jnp.int32, (dk, dk), 1)
    ones = jnp.ones((dk, dk), f32)
    gn = gn_ref[...]
    zrows = jnp.zeros((SUBLANES - 2, dk), f32)
    for h in range(heads):
        qn = _l2(c[:, h * dk:(h + 1) * dk]) * (dk ** -0.5)
        kn = _l2(c[:, bw + h * dk:bw + (h + 1) * dk])
        vh = c[:, 2 * bw + h * dk:2 * bw + (h + 1) * dk]
        beta_b = jnp.broadcast_to(beta_all[:, h:h + 1], (sb, dk))
        eg = jnp.exp(jnp.broadcast_to(g_all[:, heads + h:heads + h + 1], (sb, dk)))
        zh = zb_ref[:, h * dk:(h + 1) * dk]
        qk = jnp.sum(qn * kn, axis=-1, keepdims=True)
        for j in range(sb):
            r = slice(j, j + 1)
            s = s_ref[j, h]
            lhs = jnp.concatenate([kn[r] * beta_b[r] * eg[r], qn[r] * eg[r], zrows], axis=0)
            rs = _dot_hi(lhs, s)
            v_new = vh[r] * beta_b[r] - rs[0:1]
            o = rs[1:2] + qk[r] * v_new
            kcol = _dot_hi(jnp.where(ri == ci, jnp.broadcast_to(kn[r], (dk, dk)), 0.0), ones)
            so_ref[j, h] = s * eg[r] + kcol * v_new
            on = _rms(o) * gn * _silu(zh[r])
            o_ref[r, h * dk:(h + 1) * dk] = on.astype(o_ref.dtype)


def _gdn_sample(cb, bufs, ba, zb, state, conv_w, pa, pb, gn, heads, dk):
    dec, conv_ch = cb.shape
    bw = heads * dk
    sb = SUBLANES
    row = lambda i: (i, 0)
    const = lambda i: (0, 0)
    st = lambda i: (i, 0, 0, 0)
    return pl.pallas_call(
        functools.partial(_gdn_sample_kernel, heads=heads, dk=dk),
        grid=(dec // sb,),
        in_specs=[
            pl.BlockSpec((sb, conv_ch), row),
            pl.BlockSpec((sb, conv_ch), row),
            pl.BlockSpec((sb, conv_ch), row),
            pl.BlockSpec((sb, conv_ch), row),
            pl.BlockSpec((sb, LANES), row),
            pl.BlockSpec((sb, bw), row),
            pl.BlockSpec((sb, heads, dk, dk), st),
            pl.BlockSpec(conv_w.shape, const),
            pl.BlockSpec((1, LANES), const),
            pl.BlockSpec((1, LANES), const),
            pl.BlockSpec((1, dk), const),
        ],
        out_specs=(pl.BlockSpec((sb, bw), row), pl.BlockSpec((sb, heads, dk, dk), st)),
        out_shape=(jax.ShapeDtypeStruct((dec, bw), bf16), jax.ShapeDtypeStruct(state.shape, f32)),
        compiler_params=_params(("arbitrary",)),
        name="gdn_sample",
    )(cb, bufs[0], bufs[1], bufs[2], ba, zb, state, conv_w, pa, pb, gn)


def _merge_kernel(oa_ref, ob_ref, ga_ref, gb_ref, x_ref, gt_ref, gpm_ref, wa_ref, wb_ref, wo_ref, o_ref,
                  *, oa_transposed):
    if oa_transposed:
        ya = lax.dot_general(oa_ref[...], wa_ref[...], _TRANS_A, preferred_element_type=f32)
    else:
        ya = _dot(oa_ref[...].astype(bf16), wa_ref[...])
    yb = _dot(ob_ref[...], wb_ref[...])
    y = jax.nn.sigmoid(ga_ref[...]) * ya + jax.nn.sigmoid(gb_ref[...]) * yb
    y2 = _dot(y.astype(bf16), wo_ref[...])
    o_ref[...] = x_ref[...] + gt_ref[...] * (_rms(y2) * gpm_ref[...])


def _merge(oa, ob, ga, gb, x2, gt, gpm, wa, wb, wo, n, t, tm, oa_transposed):
    d = x2.shape[1]
    aw = wa.shape[0]
    bw = wb.shape[0]
    nt = t // tm
    row = lambda n_, i: (n_ * nt + i, 0)
    const = lambda n_, i: (0, 0)
    if oa_transposed:
        oa_spec = pl.BlockSpec((None, aw, tm), lambda n_, i: (n_, 0, i))
    else:
        oa_spec = pl.BlockSpec((tm, aw), row)
    return pl.pallas_call(
        functools.partial(_merge_kernel, oa_transposed=oa_transposed),
        grid=(n, nt),
        in_specs=[
            oa_spec,
            pl.BlockSpec((tm, bw), row),
            pl.BlockSpec((tm, d), row),
            pl.BlockSpec((tm, d), row),
            pl.BlockSpec((tm, d), row),
            pl.BlockSpec((None, gt.shape[1], d), lambda n_, i: (n_, 0, 0)),
            pl.BlockSpec((1, d), const),
            pl.BlockSpec(wa.shape, const),
            pl.BlockSpec(wb.shape, const),
            pl.BlockSpec(wo.shape, const),
        ],
        out_specs=pl.BlockSpec((tm, d), row),
        out_shape=jax.ShapeDtypeStruct(x2.shape, f32),
        compiler_params=_params(("arbitrary", "arbitrary")),
        name="merge",
    )(oa, ob, ga, gb, x2, gt, gpm, wa, wb, wo)


def _ffn_kernel(x_ref, sc_ref, sh_ref, gt_ref, gpre_ref, gpost_ref, wi_ref, wo_ref, o_ref, *, ff, fc):
    x1 = x_ref[...]
    hb = _prenorm(x1, gpre_ref[...], sc_ref[...], sh_ref[...]).astype(bf16)
    f = jnp.zeros(x1.shape, f32)
    for c in range(ff // fc):
        gate = _dot(hb, wi_ref[:, c * fc:(c + 1) * fc])
        up = _dot(hb, wi_ref[:, ff + c * fc:ff + (c + 1) * fc])
        f = f + _dot((_silu(gate) * up).astype(bf16), wo_ref[c * fc:(c + 1) * fc, :])
    o_ref[...] = x1 + gt_ref[...] * (_rms(f) * gpost_ref[...])


def _ffn(x2, sc, sh, gt, gpre, gpost, wi, wo, n, t, tm):
    d = x2.shape[1]
    ff = wo.shape[0]
    fc = ff // 2 if (ff // 2) % LANES == 0 else ff
    nt = t // tm
    row = lambda n_, i: (n_ * nt + i, 0)
    const = lambda n_, i: (0, 0)
    mod = pl.BlockSpec((None, sc.shape[1], d), lambda n_, i: (n_, 0, 0))
    return pl.pallas_call(
        functools.partial(_ffn_kernel, ff=ff, fc=fc),
        grid=(n, nt),
        in_specs=[
            pl.BlockSpec((tm, d), row), mod, mod, mod,
            pl.BlockSpec((1, d), const),
            pl.BlockSpec((1, d), const),
            pl.BlockSpec(wi.shape, const),
            pl.BlockSpec(wo.shape, const),
        ],
        out_specs=pl.BlockSpec((tm, d), row),
        out_shape=jax.ShapeDtypeStruct(x2.shape, f32),
        compiler_params=_params(("arbitrary", "arbitrary")),
        name="ffn",
    )(x2, sc, sh, gt, gpre, gpost, wi, wo)


def kernel(x_prompt, x_sample, cache_k, cache_v, state_gdn, state_conv, page_table, c_prompt, c_sample,
           w_ada, b_ada, g_pre_mix, g_post_mix, g_pre_ffn, g_post_ffn, w_in, conv_w, a_log, dt_bias,
           gdn_norm_g, w_branch_a, w_branch_b, w_out, w_ffn_in, w_ffn_out):
    depth = w_in.shape[0]
    n, t, d = x_prompt.shape
    dec, dec_t, _ = x_sample.shape
    heads, hd = cache_k.shape[3], cache_k.shape[4]
    b_heads, dk = state_gdn.shape[2], state_gdn.shape[3]
    aw, bw = heads * hd, b_heads * dk
    conv_ch = conv_w.shape[2]
    taps = conv_w.shape[1]
    assert dec_t == 1 and heads == SUBLANES and 2 * hd == LANES and dk == LANES
    assert t % MOBA_BLOCK == 0 and MOBA_BLOCK % Q_BLOCK == 0 and dec % SUBLANES == 0 and taps == 4

    xp, xs = x_prompt, x_sample.reshape(dec, d)
    outs = {k: [] for k in ("kp", "vp", "ks", "vs", "sp", "ss", "bp", "bs")}
    rows = -(-(n + dec) // SUBLANES) * SUBLANES
    c_all = jnp.concatenate([c_prompt, c_sample, jnp.zeros((rows - n - dec, d), f32)], axis=0)

    for l in range(depth):
        mod = _ada(c_all, w_ada[l], b_ada[l])
        mods = [mod[:, j * d:(j + 1) * d] for j in range(N_MOD)]
        mp = [m[:n].reshape(n, 1, d) for m in mods]
        ms = [m[n:n + dec].reshape(1, dec, d) for m in mods]
        w, cols = _layout_w_in(w_in[l], aw, conv_ch, bw, b_heads, d)
        row = lambda a: a.reshape(1, -1)
        g1, g2, g3, g4 = row(g_pre_mix[l]), row(g_post_mix[l]), row(g_pre_ffn[l]), row(g_post_ffn[l])
        wa, wb, wo = w_branch_a[l].astype(bf16), w_branch_b[l].astype(bf16), w_out[l].astype(bf16)
        wi, wf = w_ffn_in[l].astype(bf16), w_ffn_out[l].astype(bf16)
        lane_pad = lambda a: jnp.zeros((1, LANES), f32).at[0, b_heads:2 * b_heads].set(a)
        pa, pb = lane_pad(a_log[l]), lane_pad(dt_bias[l])
        gn = row(gdn_norm_g[l])

        kt, vt, q16, k16, vt16, km, cb, zb, ga, gb, ba = _proj_prompt(xp, mp[1], mp[0], g1, w, cols, heads, hd,
                                                                      conv_ch, bw)
        oa_t = _moba_prompt(q16, k16, vt16, km, n, t, heads, hd)
        ob, s_fin = _gdn_prompt(cb, ba, zb, conv_w[l], pa, pb, gn, n, t, b_heads, dk)
        x2 = xp.reshape(n * t, d)
        x1 = _merge(oa_t, ob, ga, gb, x2, mp[2], g2, wa, wb, wo, n, t, MOBA_BLOCK, True)
        xp = _ffn(x1, mp[4], mp[3], mp[5], g3, g4, wi, wf, n, t, MOBA_BLOCK).reshape(n, t, d)
        outs["kp"].append(jnp.transpose(kt.reshape(n, heads, hd, t), (0, 3, 1, 2)))
        outs["vp"].append(jnp.transpose(vt.reshape(n, heads, hd, t), (0, 3, 1, 2)))
        outs["sp"].append(s_fin)
        outs["bp"].append(cb.reshape(n, t, conv_ch)[:, t - (taps - 1):])

        q_s, k_s, v_s, cb_s, zb_s, ga_s, gb_s, ba_s = _proj_sample(xs, ms[1][0], ms[0][0], g1, w, cols)
        oa_s = _moba_sample(q_s, k_s, v_s, cache_k, cache_v, page_table, l)
        bufs = [state_conv[l][:, j] for j in range(taps - 1)]
        ob_s, s_new = _gdn_sample(cb_s, bufs, ba_s, zb_s, state_gdn[l], conv_w[l], pa, pb, gn, b_heads, dk)
        x1s = _merge(oa_s, ob_s, ga_s, gb_s, xs, ms[2], g2, wa, wb, wo, 1, dec, dec, False)
        xs = _ffn(x1s, ms[4], ms[3], ms[5], g3, g4, wi, wf, 1, dec, dec)
        outs["ks"].append(k_s.reshape(dec, 1, heads, hd))
        outs["vs"].append(v_s.reshape(dec, 1, heads, hd))
        outs["ss"].append(s_new)
        outs["bs"].append(jnp.stack([bufs[1], bufs[2], cb_s], axis=1))

    st = lambda key: jnp.stack(outs[key])
    return (xp, xs.reshape(dec, 1, d), st("kp"), st("vp"), st("ks"), st("vs"), st("sp"), st("ss"), st("bp"), st("bs"))
```

```python
import functools

import jax
import jax.numpy as jnp
from jax import lax
from jax.experimental import pallas as pl
from jax.experimental.pallas import tpu as pltpu

f32 = jnp.float32
bf16 = jnp.bfloat16

EPS = 1e-6
MOBA_BLOCK = 256
MOBA_TOPK = 3
Q_BLOCK = 128
GDN_CHUNK = 64
GDN_STEP_TOKENS = 256
N_MOD = 6

LANES = 128
SUBLANES = 8
V_PAD_ROWS = 16
MOBA_DEPTH = 4
VMEM_LIMIT_BYTES = 56 * 1024 * 1024

HIGHEST = lax.Precision.HIGHEST
NEG_INF = float("-inf")
POS_INF = float("inf")
LOG2E = 1.4426950408889634
_TRANS_B = (((1,), (1,)), ((), ()))
_TRANS_A = (((0,), (0,)), ((), ()))


def _params(semantics):
    return pltpu.CompilerParams(dimension_semantics=semantics, vmem_limit_bytes=VMEM_LIMIT_BYTES)


def _dot(a, b):
    return jnp.dot(a, b, preferred_element_type=f32)


def _dot_nt(a, b):
    return lax.dot_general(a, b, _TRANS_B, preferred_element_type=f32)


def _dot_hi(a, b):
    return jnp.dot(a, b, precision=HIGHEST, preferred_element_type=f32)


def _split(x):
    hi = x.astype(bf16)
    return hi, (x - hi.astype(f32)).astype(bf16)


def _dot3(a, b):
    ah, al = _split(a)
    bh, bl = _split(b)
    return _dot(ah, bh) + (_dot(ah, bl) + _dot(al, bh))


def _dot_mask3(mask, x):
    x1 = x.astype(bf16)
    r1 = x - x1.astype(f32)
    x2 = r1.astype(bf16)
    x3 = (r1 - x2.astype(f32)).astype(bf16)
    return _dot(mask, x1) + (_dot(mask, x2) + _dot(mask, x3))


def _silu(x):
    return x * jax.nn.sigmoid(x)


def _softplus(x):
    return jnp.maximum(x, 0.0) + jnp.log1p(jnp.exp(-jnp.abs(x)))


def _rms(x):
    return x * lax.rsqrt(jnp.mean(x * x, axis=-1, keepdims=True) + EPS)


def _l2(x):
    return x * lax.rsqrt(jnp.sum(x * x, axis=-1, keepdims=True) + EPS)


def _alibi_slope(h, heads):
    return 2.0 ** (-8.0 * (h + 1) / heads)


def _ada_kernel(c_ref, w_ref, b_ref, o_ref):
    a = _silu(c_ref[...]).astype(bf16)
    o_ref[...] = _dot(a, w_ref[...].astype(bf16)) + b_ref[...]


def _ada(c_all, w_ada, b_ada):
    rows, d = c_all.shape
    nm = w_ada.shape[1]
    tn = 512
    return pl.pallas_call(
        _ada_kernel,
        grid=(nm // tn,),
        in_specs=[
            pl.BlockSpec((rows, d), lambda j: (0, 0)),
            pl.BlockSpec((d, tn), lambda j: (0, j)),
            pl.BlockSpec((1, tn), lambda j: (0, j)),
        ],
        out_specs=pl.BlockSpec((rows, tn), lambda j: (0, j)),
        out_shape=jax.ShapeDtypeStruct((rows, nm), f32),
        compiler_params=_params(("arbitrary",)),
        name="ada",
    )(c_all, w_ada, b_ada.reshape(1, nm))


def _prenorm(x, g, sc, sh):
    return (_rms(x) * g) * (1.0 + sc) + sh


def _proj_prompt_kernel(x_ref, sc_ref, sh_ref, g_ref, w_ref,
                        kt_ref, vt_ref, q16_ref, k16_ref, vt16_ref, km_ref, cb_ref, zb_ref, ga_ref, gb_ref, ba_ref,
                        *, hd, cols):
    hb = _prenorm(x_ref[...], g_ref[...], sc_ref[...], sh_ref[...]).astype(bf16)

    def proj(name):
        a, b = cols[name]
        return _dot(hb, w_ref[:, a:b])

    q, k, v = proj("q"), proj("k"), proj("v")
    tm = x_ref.shape[0]
    q16_ref[...] = (q * (hd ** -0.5 * LOG2E)).astype(bf16)
    km_ref[0] = jnp.mean(k, axis=0, keepdims=True)
    vrows = vt16_ref.shape[1]
    ones_row = (lax.broadcasted_iota(jnp.int32, (vrows - hd, tm), 0) == 0).astype(bf16)
    for g in range(k.shape[1] // LANES):
        cs = slice(g * LANES, (g + 1) * LANES)
        k16_ref[g] = k[:, cs].astype(bf16)
        kt_ref[cs, :] = k[:, cs].T
        vt = v[:, cs].T
        vt_ref[cs, :] = vt
        for e in range(LANES // hd):
            h = g * (LANES // hd) + e
            vt16_ref[h, 0:hd, :] = vt[e * hd:(e + 1) * hd].astype(bf16)
            vt16_ref[h, hd:vrows, :] = ones_row
    cb_ref[...] = proj("cb")
    zb_ref[...] = proj("zb")
    ga_ref[...] = proj("ga")
    gb_ref[...] = proj("gb")
    ba_ref[...] = proj("ba")


def _proj_sample_kernel(x_ref, sc_ref, sh_ref, g_ref, w_ref,
                        q_ref, k_ref, v_ref, cb_ref, zb_ref, ga_ref, gb_ref, ba_ref, *, cols):
    hb = _prenorm(x_ref[...], g_ref[...], sc_ref[...], sh_ref[...]).astype(bf16)
    for name, ref in (("q", q_ref), ("k", k_ref), ("v", v_ref), ("cb", cb_ref), ("zb", zb_ref),
                      ("ga", ga_ref), ("gb", gb_ref), ("ba", ba_ref)):
        a, b = cols[name]
        ref[...] = _dot(hb, w_ref[:, a:b])


def _layout_w_in(w_in, a_width, conv_ch, b_width, b_heads, d):
    o_q, o_k, o_v = 0, a_width, 2 * a_width
    o_cb = 3 * a_width
    o_zb = o_cb + conv_ch
    o_ba = o_zb + b_width
    o_ga = o_ba + 2 * b_heads
    o_gb = o_ga + d
    main = w_in[:, :o_ba]
    gates = w_in[:, o_ga:o_gb + d]
    ba = w_in[:, o_ba:o_ga]
    pad = jnp.zeros((w_in.shape[0], LANES - 2 * b_heads), w_in.dtype)
    w = jnp.concatenate([main, gates, ba, pad], axis=1).astype(bf16)
    cols = {"q": (o_q, o_k), "k": (o_k, o_v), "v": (o_v, o_cb), "cb": (o_cb, o_zb), "zb": (o_zb, o_ba),
            "ga": (o_ba, o_ba + d), "gb": (o_ba + d, o_ba + 2 * d), "ba": (o_ba + 2 * d, o_ba + 2 * d + LANES)}
    return w, cols


def _proj_prompt(x, sc, sh, g, w, cols, heads, hd, conv_ch, b_width):
    n, t, d = x.shape
    tm = MOBA_BLOCK
    nb = t // tm
    aw = heads * hd
    wcols = w.shape[1]
    row = lambda n_, i: (n_ * nb + i, 0)
    mod = lambda n_, i: (n_, 0, 0)
    tok_minor = lambda n_, i: (n_, 0, i)
    out_shape = (
        jax.ShapeDtypeStruct((n, aw, t), f32),
        jax.ShapeDtypeStruct((n, aw, t), f32),
        jax.ShapeDtypeStruct((n * t, aw), bf16),
        jax.ShapeDtypeStruct((n, nb, aw // LANES, tm, LANES), bf16),
        jax.ShapeDtypeStruct((n, nb, heads, hd + V_PAD_ROWS, tm), bf16),
        jax.ShapeDtypeStruct((n * nb, 1, aw), f32),
        jax.ShapeDtypeStruct((n * t, conv_ch), f32),
        jax.ShapeDtypeStruct((n * t, b_width), f32),
        jax.ShapeDtypeStruct((n * t, d), f32),
        jax.ShapeDtypeStruct((n * t, d), f32),
        jax.ShapeDtypeStruct((n * t, LANES), f32),
    )
    out_specs = (
        pl.BlockSpec((None, aw, tm), tok_minor),
        pl.BlockSpec((None, aw, tm), tok_minor),
        pl.BlockSpec((tm, aw), row),
        pl.BlockSpec((None, None, aw // LANES, tm, LANES), lambda n_, i: (n_, i, 0, 0, 0)),
        pl.BlockSpec((None, None, heads, hd + V_PAD_ROWS, tm), lambda n_, i: (n_, i, 0, 0, 0)),
        pl.BlockSpec((1, 1, aw), lambda n_, i: (n_ * nb + i, 0, 0)),
        pl.BlockSpec((tm, conv_ch), row),
        pl.BlockSpec((tm, b_width), row),
        pl.BlockSpec((tm, d), row),
        pl.BlockSpec((tm, d), row),
        pl.BlockSpec((tm, LANES), row),
    )
    return pl.pallas_call(
        functools.partial(_proj_prompt_kernel, hd=hd, cols=cols),
        grid=(n, nb),
        in_specs=[
            pl.BlockSpec((tm, d), row),
            pl.BlockSpec((None, 1, d), mod),
            pl.BlockSpec((None, 1, d), mod),
            pl.BlockSpec((1, d), lambda n_, i: (0, 0)),
            pl.BlockSpec((d, wcols), lambda n_, i: (0, 0)),
        ],
        out_specs=out_specs,
        out_shape=out_shape,
        compiler_params=_params(("arbitrary", "arbitrary")),
        name="proj_prompt",
    )(x.reshape(n * t, d), sc, sh, g, w)


def _proj_sample(x, sc, sh, g, w, cols):
    rows, d = x.shape
    wcols = w.shape[1]
    names = ("q", "k", "v", "cb", "zb", "ga", "gb", "ba")
    widths = [cols[nm][1] - cols[nm][0] for nm in names]
    full = lambda i: (0, 0)
    return pl.pallas_call(
        functools.partial(_proj_sample_kernel, cols=cols),
        grid=(1,),
        in_specs=[
            pl.BlockSpec((rows, d), full),
            pl.BlockSpec((rows, d), full),
            pl.BlockSpec((rows, d), full),
            pl.BlockSpec((1, d), full),
            pl.BlockSpec((d, wcols), full),
        ],
        out_specs=tuple(pl.BlockSpec((rows, wd), full) for wd in widths),
        out_shape=tuple(jax.ShapeDtypeStruct((rows, wd), f32) for wd in widths),
        compiler_params=_params(("arbitrary",)),
        name="proj_sample",
    )(x, sc, sh, g, w)


def _moba_prompt_kernel(q_ref, k_ref, vt_ref, kmh_ref, kml_ref, bias_ref, o_ref,
                        qq_ref, sel_ref, m_ref, acc_ref, s_ref, st_ref, *, heads, hd, topk):
    c = pl.program_id(1)
    qb = q_ref.shape[0]
    nbp = kmh_ref.shape[0]
    mb = k_ref.shape[2]
    per = LANES // hd
    pairs = heads // per
    own = (c * qb) // mb
    lane = lax.broadcasted_iota(jnp.int32, (qb, LANES), 1)
    bid = lax.broadcasted_iota(jnp.int32, (nbp, qb), 0)
    kr = lax.broadcasted_iota(jnp.int32, (mb, qb), 0)
    ql = lax.broadcasted_iota(jnp.int32, (mb, qb), 1)
    causal = kr <= ql + (c * qb - own * mb)

    for g in range(pairs):
        pair = slice(g * LANES, (g + 1) * LANES)
        qpair = q_ref[:, pair]
        for e in range(per):
            keep = jnp.logical_and(lane >= e * hd, lane < (e + 1) * hd)
            qq_ref[g, e * qb:(e + 1) * qb, :] = jnp.where(keep, qpair, jnp.zeros_like(qpair))
        qq = qq_ref[g]
        bs2 = _dot_nt(kmh_ref[:, pair], qq) + _dot_nt(kml_ref[:, pair], qq)
        s2 = _dot_nt(k_ref[own, g], qq)
        for e in range(per):
            h = g * per + e
            bs = jnp.where(bid < own, bs2[:, e * qb:(e + 1) * qb], NEG_INF)
            sel = jnp.zeros((nbp, qb), f32)
            for t in range(topk):
                mx = jnp.max(bs, axis=0, keepdims=True)
                idx = jnp.min(jnp.where(bs == mx, bid, nbp), axis=0, keepdims=True)
                hit = bid == idx
                sel = jnp.where(jnp.logical_and(hit, t < own), 1.0, sel)
                bs = jnp.where(hit, NEG_INF, bs)
            sel_ref[h] = sel
            s = jnp.where(causal, s2[:, e * qb:(e + 1) * qb] + bias_ref[h], NEG_INF)
            m0 = jnp.max(s, axis=0, keepdims=True)
            m_ref[h] = m0
            acc_ref[h] = _dot(vt_ref[own, h], jnp.exp2(s - m0).astype(bf16))

    last = k_ref.shape[0] - 1

    def scores(j, slot):
        j = jnp.minimum(j, last)
        for g in range(pairs):
            s2 = _dot_nt(k_ref[j, g], qq_ref[g])
            for e in range(per):
                h = g * per + e
                on = sel_ref[h, pl.ds(j, 1), :] > 0.0
                s = s2[:, e * qb:(e + 1) * qb] + bias_ref[h]
                s_ref[slot, h] = s
                off = jnp.full((1, qb), j - own, jnp.int32).astype(f32) * (_alibi_slope(h, heads) * mb * LOG2E)
                m = m_ref[h]
                m_new = jnp.where(on, jnp.maximum(m, jnp.max(s, axis=0, keepdims=True) + off), m)
                st_ref[slot, h] = jnp.concatenate([jnp.exp2(m - m_new), jnp.where(on, m_new - off, POS_INF)], axis=0)
                m_ref[h] = m_new

    def consume(j, slot):
        j = jnp.minimum(j, last)
        for h in range(heads):
            st = st_ref[slot, h]
            p = jnp.exp2(s_ref[slot, h] - st[1:2])
            acc_ref[h] = st[0:1] * acc_ref[h] + _dot(vt_ref[j, h], p.astype(bf16))

    depth = s_ref.shape[0] // 2
    for u in range(depth):
        scores(u, u)

    def body(i, carry):
        base = depth * lax.rem(i, 2)
        for u in range(depth):
            consume(depth * i + u, base + u)
        for u in range(depth):
            scores(depth * (i + 1) + u, depth - base + u)
        return carry

    trips = (own + depth - 1) // depth
    lax.fori_loop(0, trips - 1, body, 0)
    final = jnp.maximum(trips - 1, 0)
    for u in range(depth):
        consume(depth * final + u, depth * lax.rem(final, 2) + u)
    for h in range(heads):
        acc = acc_ref[h]
        o_ref[h * hd:(h + 1) * hd, :] = (acc[:hd] / acc[hd:hd + 1]).astype(o_ref.dtype)


def _moba_prompt(q16, k16, vt16, km, n, t, heads, hd):
    aw = heads * hd
    nb = t // MOBA_BLOCK
    nq = t // Q_BLOCK
    per = LANES // hd
    pairs = heads // per
    vrows = vt16.shape[3]
    nbp = -(-nb // SUBLANES) * SUBLANES
    assert all(_alibi_slope(h, heads) * MOBA_BLOCK == int(_alibi_slope(h, heads) * MOBA_BLOCK) for h in range(heads))
    km = jnp.pad(km.reshape(n, nb, aw), ((0, 0), (0, nbp - nb), (0, 0)))
    km_hi = km.astype(bf16)
    km_lo = (km - km_hi.astype(f32)).astype(bf16)
    pos = jnp.arange(MOBA_BLOCK, dtype=f32)[None, :, None]
    slopes = jnp.asarray([_alibi_slope(h, heads) for h in range(heads)], f32)[:, None, None]
    bias = jnp.broadcast_to(slopes * pos * LOG2E, (heads, MOBA_BLOCK, Q_BLOCK))
    whole = lambda n_, c: (n_, 0, 0)
    resident = dict(pipeline_mode=pl.Buffered(1))
    return pl.pallas_call(
        functools.partial(_moba_prompt_kernel, heads=heads, hd=hd, topk=MOBA_TOPK),
        grid=(n, nq),
        in_specs=[
            pl.BlockSpec((None, Q_BLOCK, aw), lambda n_, c: (n_, c, 0)),
            pl.BlockSpec((None, nb, pairs, MOBA_BLOCK, LANES), lambda n_, c: (n_, 0, 0, 0, 0), **resident),
            pl.BlockSpec((None, nb, heads, vrows, MOBA_BLOCK), lambda n_, c: (n_, 0, 0, 0, 0), **resident),
            pl.BlockSpec((None, nbp, aw), whole),
            pl.BlockSpec((None, nbp, aw), whole),
            pl.BlockSpec((heads, MOBA_BLOCK, Q_BLOCK), lambda n_, c: (0, 0, 0)),
        ],
        out_specs=pl.BlockSpec((None, aw, Q_BLOCK), lambda n_, c: (n_, 0, c)),
        out_shape=jax.ShapeDtypeStruct((n, aw, t), bf16),
        scratch_shapes=[
            pltpu.VMEM((pairs, per * Q_BLOCK, LANES), bf16),
            pltpu.VMEM((heads, nbp, Q_BLOCK), f32),
            pltpu.VMEM((heads, 1, Q_BLOCK), f32),
            pltpu.VMEM((heads, vrows, Q_BLOCK), f32),
            pltpu.VMEM((2 * MOBA_DEPTH, heads, MOBA_BLOCK, Q_BLOCK), f32),
            pltpu.VMEM((2 * MOBA_DEPTH, heads, 2, Q_BLOCK), f32),
        ],
        compiler_params=_params(("arbitrary", "arbitrary")),
        name="moba_prompt",
    )(q16.reshape(n, t, aw), k16, vt16, km_hi, km_lo, bias)


def _sel_sample_kernel(pt_ref, q_ref, kn_ref, ck_ref, sel_ref, buf_ref, sem, *, layer, n_pages, ppb, topk, own):
    i = pl.program_id(0)
    slot = lax.rem(i, 2)

    def page_copy(sample, slot_, p):
        return pltpu.make_async_copy(ck_ref.at[layer, pt_ref[sample * n_pages + p]], buf_ref.at[slot_, p],
                                     sem.at[slot_])

    @pl.when(i == 0)
    def _():
        for p in range(n_pages):
            page_copy(0, 0, p).start()

    @pl.when(i + 1 < pl.num_programs(0))
    def _():
        for p in range(n_pages):
            page_copy(i + 1, 1 - slot, p).start()

    for p in range(n_pages):
        page_copy(i, slot, p).wait()

    heads, hd = q_ref.shape
    nbk = n_pages // ppb
    lane = lax.broadcasted_iota(jnp.int32, (hd, LANES), 1)
    q = q_ref[...]
    rows = []
    for h in range(heads):
        def add_block(b, m, h=h):
            s = buf_ref[slot, ppb * b, h]
            for j in range(1, ppb):
                s = s + buf_ref[slot, ppb * b + j, h]
            return jnp.where(lane == b, jnp.sum(s, axis=-1, keepdims=True), m)

        ksum_t = lax.fori_loop(0, nbk, add_block, jnp.zeros((hd, LANES), f32), unroll=min(8, nbk))
        rows.append(_dot_hi(q, ksum_t)[h:h + 1])
    lane8 = lax.broadcasted_iota(jnp.int32, (heads, LANES), 1)
    bs = jnp.concatenate(rows, axis=0) / MOBA_BLOCK
    sc_new = jnp.sum(kn_ref[...] * q, axis=-1, keepdims=True) / MOBA_BLOCK
    bs = jnp.where(lane8 == own, sc_new, bs)
    bs = jnp.where(lane8 < own, bs, NEG_INF)
    out = jnp.zeros((heads, LANES), jnp.int32)
    for t in range(topk):
        mx = jnp.max(bs, axis=-1, keepdims=True)
        idx = jnp.min(jnp.where(bs == mx, lane8, LANES), axis=-1, keepdims=True)
        out = jnp.where(lane8 == t, idx, out)
        bs = jnp.where(lane8 == idx, NEG_INF, bs)
    sel_ref[...] = out


def _attn_sample_kernel(pt_ref, sel_ref, q_ref, kn_ref, vn_ref, ck_ref, cv_ref, o_ref, kbuf, vbuf, sem,
                        *, layer, n_pages, ppb, topk, own, past):
    i = pl.program_id(0)
    slot = lax.rem(i, 2)
    heads, hd = q_ref.shape
    page = kbuf.shape[-1]

    def slab_copies(sample, slot_):
        out = []
        for h in range(heads):
            for t in range(topk):
                blk = sel_ref[(sample * heads + h) * topk + t]
                for j in range(ppb):
                    pg = pt_ref[sample * n_pages + ppb * blk + j]
                    r = (h * topk + t) * ppb + j
                    out.append(pltpu.make_async_copy(ck_ref.at[layer, pg, h], kbuf.at[slot_, r], sem.at[0, slot_]))
                    out.append(pltpu.make_async_copy(cv_ref.at[layer, pg, h], vbuf.at[slot_, r], sem.at[1, slot_]))
        return out

    @pl.when(i == 0)
    def _():
        for cp in slab_copies(0, 0):
            cp.start()

    @pl.when(i + 1 < pl.num_programs(0))
    def _():
        for cp in slab_copies(i + 1, 1 - slot):
            cp.start()

    for cp in slab_copies(i, slot):
        cp.wait()

    scale = hd ** -0.5
    q = q_ref[...]
    qb = (q * scale).astype(bf16)
    lane = lax.broadcasted_iota(jnp.int32, (1, page), 1)
    for h in range(heads):
        slope = _alibi_slope(h, heads)
        pieces = []
        for t in range(topk):
            blk = sel_ref[(i * heads + h) * topk + t]
            for j in range(ppb):
                r = (h * topk + t) * ppb + j
                if t < own:
                    s = _dot(qb, kbuf[slot, r].astype(bf16))[h:h + 1]
                    pos = blk * MOBA_BLOCK + j * page + lane
                    pieces.append(s - slope * (past - pos).astype(f32))
                else:
                    pieces.append(jnp.full((1, page), NEG_INF, f32))
        s_sel = jnp.concatenate(pieces, axis=1)
        s_own = jnp.sum(q[h:h + 1] * kn_ref[h:h + 1, :], axis=-1, keepdims=True) * scale
        m = jnp.maximum(jnp.max(s_sel, axis=-1, keepdims=True), s_own)
        p = jnp.exp(s_sel - m)
        p_own = jnp.exp(s_own - m)
        den = jnp.sum(p, axis=-1, keepdims=True) + p_own
        num = p_own * vn_ref[h:h + 1, :]
        for x in range(topk * ppb):
            px = jnp.broadcast_to(p[:, x * page:(x + 1) * page], (SUBLANES, page)).astype(bf16)
            num = num + _dot_nt(px, vbuf[slot, h * topk * ppb + x].astype(bf16))[0:1]
        o_ref[h:h + 1, :] = num / den


def _moba_sample(q_s, k_s, v_s, cache_k, cache_v, page_table, layer):
    dec = q_s.shape[0]
    _, n_pool, page, heads, hd = cache_k.shape
    n_pages = page_table.shape[1]
    past = n_pages * page
    ppb = MOBA_BLOCK // page
    assert MOBA_BLOCK % page == 0 and past % MOBA_BLOCK == 0 and page == LANES
    own = past // MOBA_BLOCK
    assert own < LANES
    topk = min(MOBA_TOPK, own + 1)
    q3 = q_s.reshape(dec, heads, hd)
    kn3 = k_s.reshape(dec, heads, hd)
    vn3 = v_s.reshape(dec, heads, hd)
    ck = jnp.transpose(cache_k, (0, 1, 3, 4, 2))
    cv = jnp.transpose(cache_v, (0, 1, 3, 4, 2))
    pt_flat = page_table.reshape(-1).astype(jnp.int32)
    per_sample = lambda i, *_: (i, 0, 0)
    hbm = pl.BlockSpec(memory_space=pl.ANY)

    sel = pl.pallas_call(
        functools.partial(_sel_sample_kernel, layer=layer, n_pages=n_pages, ppb=ppb, topk=topk, own=own),
        grid_spec=pltpu.PrefetchScalarGridSpec(
            num_scalar_prefetch=1,
            grid=(dec,),
            in_specs=[pl.BlockSpec((None, heads, hd), per_sample), pl.BlockSpec((None, heads, hd), per_sample), hbm],
            out_specs=pl.BlockSpec((None, heads, LANES), per_sample),
            scratch_shapes=[pltpu.VMEM((2, n_pages, heads, hd, page), f32), pltpu.SemaphoreType.DMA((2,))],
        ),
        out_shape=jax.ShapeDtypeStruct((dec, heads, LANES), jnp.int32),
        compiler_params=_params(("arbitrary",)),
        name="moba_sample_select",
    )(pt_flat, q3, kn3, ck)
    sel_flat = sel[:, :, :topk].reshape(-1)

    n_slabs = heads * topk * ppb
    o3 = pl.pallas_call(
        functools.partial(_attn_sample_kernel, layer=layer, n_pages=n_pages, ppb=ppb, topk=topk, own=own, past=past),
        grid_spec=pltpu.PrefetchScalarGridSpec(
            num_scalar_prefetch=2,
            grid=(dec,),
            in_specs=[pl.BlockSpec((None, heads, hd), per_sample), pl.BlockSpec((None, heads, hd), per_sample),
                      pl.BlockSpec((None, heads, hd), per_sample), hbm, hbm],
            out_specs=pl.BlockSpec((None, heads, hd), per_sample),
            scratch_shapes=[pltpu.VMEM((2, n_slabs, hd, page), f32), pltpu.VMEM((2, n_slabs, hd, page), f32),
                            pltpu.SemaphoreType.DMA((2, 2))],
        ),
        out_shape=jax.ShapeDtypeStruct((dec, heads, hd), f32),
        compiler_params=_params(("arbitrary",)),
        name="moba_sample_attend",
    )(pt_flat, sel_flat, q3, kn3, vn3, ck, cv)
    return o3.reshape(dec, heads * hd)


def _gdn_masks(td, ch):
    ri = lax.broadcasted_iota(jnp.int32, (td, td), 0)
    ci = lax.broadcasted_iota(jnp.int32, (td, td), 1)
    same = ri // ch == ci // ch
    masks = [jnp.logical_and(same, ri >= ci), jnp.logical_and(same, ri > ci)]
    b = 1
    while b < ch:
        masks.append(jnp.logical_and(jnp.logical_and(ri // (2 * b) == ci // (2 * b), ri % (2 * b) >= b),
                                     ci % (2 * b) < b))
        b *= 2
    return [m.astype(f32) for m in masks]


def _bdot(a, b):
    return jnp.einsum("bij,bjk->bik", a, b, preferred_element_type=f32)


def _bdot_nt(a, b):
    return jnp.einsum("bik,bjk->bij", a, b, preferred_element_type=f32)


def _bdot_tn(a, b):
    return jnp.einsum("bki,bkj->bij", a, b, preferred_element_type=f32)


def _bdot3(a, b):
    ah, al = _split(a)
    bh, bl = _split(b)
    return _bdot(ah, bh) + (_bdot(ah, bl) + _bdot(al, bh))


def _gdn_prompt_kernel(cb_ref, ba_ref, zb_ref, cw_ref, pa_ref, pb_ref, gn_ref, o_ref, sfin_ref,
                       xp_ref, s_ref, mk_ref, *, heads, dk, ch, taps):
    nseq, tt = cb_ref.shape[0], cb_ref.shape[1]
    td = mk_ref.shape[1]
    nt = tt // td
    bw = heads * dk
    pad = SUBLANES
    nlev = mk_ref.shape[0] - 2

    @pl.when(pl.program_id(0) == 0)
    def _():
        for x, m in enumerate(_gdn_masks(td, ch)):
            mk_ref[x] = m
        xp_ref[:, 0:pad, :] = jnp.zeros((nseq, pad, xp_ref.shape[2]), f32)
        s_ref[...] = jnp.zeros_like(s_ref)

    w = cw_ref[...]
    gn = gn_ref[...]
    chains = [(b, h, x) for b in range(nseq) for h in range(heads) for x in range(nt)]
    q_l, k_l, v_l, beta_l, g_l, z_l = [], [], [], [], [], []
    for b in range(nseq):
        xp_ref[b, pad:pad + tt, :] = cb_ref[b]
        y = w[0:1] * xp_ref[b, pad - (taps - 1):pad - (taps - 1) + tt, :]
        for j in range(1, taps):
            y = y + w[j:j + 1] * xp_ref[b, pad - (taps - 1) + j:pad - (taps - 1) + j + tt, :]
        xp_ref[b, 0:pad, :] = xp_ref[b, tt:tt + pad, :]
        c = _silu(y)
        ba = ba_ref[b]
        beta_all = jax.nn.sigmoid(ba)
        g_all = -jnp.exp(pa_ref[...]) * _softplus(ba + pb_ref[...])
        for h in range(heads):
            for x in range(nt):
                rows = slice(x * td, (x + 1) * td)
                q_l.append(c[rows, h * dk:(h + 1) * dk])
                k_l.append(c[rows, bw + h * dk:bw + (h + 1) * dk])
                v_l.append(c[rows, 2 * bw + h * dk:2 * bw + (h + 1) * dk])
                beta_l.append(jnp.broadcast_to(beta_all[rows, h:h + 1], (td, dk)))
                g_l.append(jnp.broadcast_to(g_all[rows, heads + h:heads + h + 1], (td, dk)))
                z_l.append(zb_ref[b, rows, h * dk:(h + 1) * dk])
    nc = len(chains)
    qn = _l2(jnp.stack(q_l)) * (dk ** -0.5)
    kn = _l2(jnp.stack(k_l))
    vh = jnp.stack(v_l)
    beta_b = jnp.stack(beta_l)
    incl, strict = mk_ref[0], mk_ref[1]
    gc2 = _dot_mask3(incl.astype(bf16), jnp.concatenate(g_l, axis=1))
    gcum = jnp.stack([gc2[:, x * dk:(x + 1) * dk] for x in range(nc)])
    gcol = jnp.concatenate([gcum] * (td // dk), axis=2) if td != dk else gcum
    diff = (gcol - jnp.swapaxes(gcol, 1, 2)) * incl
    decay = jnp.exp(diff) * incl
    eg = jnp.exp(gcum)
    kb = kn * beta_b
    kb16, kn16 = kb.astype(bf16), kn.astype(bf16)
    a = _bdot_nt(kb16, kn16) * decay * strict
    e = -(a * mk_ref[2])
    for lv in range(1, nlev):
        r = a * mk_ref[2 + lv]
        y2 = r + _bdot(e.astype(bf16), r.astype(bf16))
        e = e - y2 - _bdot(y2.astype(bf16), e.astype(bf16))
    rhs = jnp.concatenate([vh * beta_b, kb * eg], axis=2)
    uw = rhs + _bdot3(e, rhs)
    u, wv16 = uw[:, :, :dk], uw[:, :, dk:].astype(bf16)
    qk = _bdot_nt(qn.astype(bf16), kn16) * decay
    g_end = jnp.concatenate([jnp.broadcast_to(gcum[:, (x + 1) * ch - 1:(x + 1) * ch, :], (nc, ch, dk))
                             for x in range(td // ch)], axis=1)
    k_dec = (kn * jnp.exp(g_end - gcum)).astype(bf16)
    q_dec = (qn * eg).astype(bf16)
    g_last = jnp.exp(g_end)

    ns = nseq * heads
    pick = lambda arr, x: arr.reshape((ns, nt) + arr.shape[1:])[:, x]
    s = s_ref[...].reshape(ns, dk, dk)
    v_new = [[None] * (td // ch) for _ in range(nt)]
    o_int = [[None] * (td // ch) for _ in range(nt)]
    for x in range(nt):
        u_x, w_x, qd_x, kd_x, gl_x = pick(u, x), pick(wv16, x), pick(q_dec, x), pick(k_dec, x), pick(g_last, x)
        for cix in range(td // ch):
            sl = slice(cix * ch, (cix + 1) * ch)
            sb = s.astype(bf16)
            vn = u_x[:, sl] - _bdot(w_x[:, sl], sb)
            o_int[x][cix] = _bdot(qd_x[:, sl], sb)
            s = s * gl_x[:, (cix + 1) * ch - 1:(cix + 1) * ch, :] + _bdot_tn(kd_x[:, sl], vn.astype(bf16))
            v_new[x][cix] = vn
    s_ref[...] = s.reshape(nseq, heads, dk, dk)
    sfin_ref[...] = s.reshape(nseq, heads, dk, dk)
    vn_all = jnp.stack([jnp.concatenate(v_new[x], axis=1) for x in range(nt)], axis=1).reshape(nc, td, dk)
    oi_all = jnp.stack([jnp.concatenate(o_int[x], axis=1) for x in range(nt)], axis=1).reshape(nc, td, dk)
    o = oi_all + _bdot(qk.astype(bf16), vn_all.astype(bf16))
    on = (_rms(o) * gn * _silu(jnp.stack(z_l))).astype(o_ref.dtype)
    for i, (b, h, x) in enumerate(chains):
        o_ref[b, x * td:(x + 1) * td, h * dk:(h + 1) * dk] = on[i]


def _gdn_prompt(cb, ba, zb, conv_w, pa, pb, gn, n, t, heads, dk):
    td = dk
    tt = GDN_STEP_TOKENS if t % GDN_STEP_TOKENS == 0 else td
    conv_ch = cb.shape[1]
    bw = heads * dk
    nt = t // tt
    ch = GDN_CHUNK
    assert t % tt == 0 and tt % td == 0 and td % ch == 0 and ch & (ch - 1) == 0
    nmask = 2 + ch.bit_length() - 1
    tile = lambda i: (0, i, 0)
    const = lambda i: (0, 0)
    o, s_fin = pl.pallas_call(
        functools.partial(_gdn_prompt_kernel, heads=heads, dk=dk, ch=ch, taps=conv_w.shape[0]),
        grid=(nt,),
        in_specs=[
            pl.BlockSpec((n, tt, conv_ch), tile),
            pl.BlockSpec((n, tt, LANES), tile),
            pl.BlockSpec((n, tt, bw), tile),
            pl.BlockSpec(conv_w.shape, const),
            pl.BlockSpec((1, LANES), const),
            pl.BlockSpec((1, LANES), const),
            pl.BlockSpec((1, dk), const),
        ],
        out_specs=(
            pl.BlockSpec((n, tt, bw), tile),
            pl.BlockSpec((n, heads, dk, dk), lambda i: (0, 0, 0, 0)),
        ),
        out_shape=(
            jax.ShapeDtypeStruct((n, t, bw), bf16),
            jax.ShapeDtypeStruct((n, heads, dk, dk), f32),
        ),
        scratch_shapes=[pltpu.VMEM((n, tt + SUBLANES, conv_ch), f32), pltpu.VMEM((n, heads, dk, dk), f32),
                        pltpu.VMEM((nmask, td, td), f32)],
        compiler_params=_params(("arbitrary",)),
        name="gdn_prompt",
    )(cb.reshape(n, t, conv_ch), ba.reshape(n, t, LANES), zb.reshape(n, t, bw), conv_w, pa, pb, gn)
    return o.reshape(n * t, bw), s_fin


def _gdn_sample_kernel(cb_ref, b0_ref, b1_ref, b2_ref, ba_ref, zb_ref, s_ref, cw_ref, pa_ref, pb_ref, gn_ref,
                       o_ref, so_ref, *, heads, dk):
    sb = cb_ref.shape[0]
    bw = heads * dk
    w = cw_ref[...]
    y = w[0:1] * b0_ref[...]
    y = y + w[1:2] * b1_ref[...]
    y = y + w[2:3] * b2_ref[...]
    y = y + w[3:4] * cb_ref[...]
    c = _silu(y)
    ba = ba_ref[...]
    beta_all = jax.nn.sigmoid(ba)
    g_all = -jnp.exp(pa_ref[...]) * _softplus(ba + pb_ref[...])
    ri = lax.broadcasted_iota(jnp.int32, (dk, dk), 0)
    ci = lax.broadcasted_iota(jnp.int32, (dk, dk), 1)
    ones = jnp.ones((dk, dk), f32)
    gn = gn_ref[...]
    zrows = jnp.zeros((SUBLANES - 2, dk), f32)
    for h in range(heads):
        qn = _l2(c[:, h * dk:(h + 1) * dk]) * (dk ** -0.5)
        kn = _l2(c[:, bw + h * dk:bw + (h + 1) * dk])
        vh = c[:, 2 * bw + h * dk:2 * bw + (h + 1) * dk]
        beta_b = jnp.broadcast_to(beta_all[:, h:h + 1], (sb, dk))
        eg = jnp.exp(jnp.broadcast_to(g_all[:, heads + h:heads + h + 1], (sb, dk)))
        zh = zb_ref[:, h * dk:(h + 1) * dk]
        qk = jnp.sum(qn * kn, axis=-1, keepdims=True)
        for j in range(sb):
            r = slice(j, j + 1)
            s = s_ref[j, h]
            lhs = jnp.concatenate([kn[r] * beta_b[r] * eg[r], qn[r] * eg[r], zrows], axis=0)
            rs = _dot_hi(lhs, s)
            v_new = vh[r] * beta_b[r] - rs[0:1]
            o = rs[1:2] + qk[r] * v_new
            kcol = _dot_hi(jnp.where(ri == ci, jnp.broadcast_to(kn[r], (dk, dk)), 0.0), ones)
            so_ref[j, h] = s * eg[r] + kcol * v_new
            on = _rms(o) * gn * _silu(zh[r])
            o_ref[r, h * dk:(h + 1) * dk] = on.astype(o_ref.dtype)


def _gdn_sample(cb, bufs, ba, zb, state, conv_w, pa, pb, gn, heads, dk):
    dec, conv_ch = cb.shape
    bw = heads * dk
    sb = SUBLANES
    row = lambda i: (i, 0)
    const = lambda i: (0, 0)
    st = lambda i: (i, 0, 0, 0)
    return pl.pallas_call(
        functools.partial(_gdn_sample_kernel, heads=heads, dk=dk),
        grid=(dec // sb,),
        in_specs=[
            pl.BlockSpec((sb, conv_ch), row),
            pl.BlockSpec((sb, conv_ch), row),
            pl.BlockSpec((sb, conv_ch), row),
            pl.BlockSpec((sb, conv_ch), row),
            pl.BlockSpec((sb, LANES), row),
            pl.BlockSpec((sb, bw), row),
            pl.BlockSpec((sb, heads, dk, dk), st),
            pl.BlockSpec(conv_w.shape, const),
            pl.BlockSpec((1, LANES), const),
            pl.BlockSpec((1, LANES), const),
            pl.BlockSpec((1, dk), const),
        ],
        out_specs=(pl.BlockSpec((sb, bw), row), pl.BlockSpec((sb, heads, dk, dk), st)),
        out_shape=(jax.ShapeDtypeStruct((dec, bw), bf16), jax.ShapeDtypeStruct(state.shape, f32)),
        compiler_params=_params(("arbitrary",)),
        name="gdn_sample",
    )(cb, bufs[0], bufs[1], bufs[2], ba, zb, state, conv_w, pa, pb, gn)


def _merge_kernel(oa_ref, ob_ref, ga_ref, gb_ref, x_ref, gt_ref, gpm_ref, wa_ref, wb_ref, wo_ref, o_ref,
                  *, oa_transposed):
    if oa_transposed:
        ya = lax.dot_general(oa_ref[...], wa_ref[...], _TRANS_A, preferred_element_type=f32)
    else:
        ya = _dot(oa_ref[...].astype(bf16), wa_ref[...])
    yb = _dot(ob_ref[...], wb_ref[...])
    y = jax.nn.sigmoid(ga_ref[...]) * ya + jax.nn.sigmoid(gb_ref[...]) * yb
    y2 = _dot(y.astype(bf16), wo_ref[...])
    o_ref[...] = x_ref[...] + gt_ref[...] * (_rms(y2) * gpm_ref[...])


def _merge(oa, ob, ga, gb, x2, gt, gpm, wa, wb, wo, n, t, tm, oa_transposed):
    d = x2.shape[1]
    aw = wa.shape[0]
    bw = wb.shape[0]
    nt = t // tm
    row = lambda n_, i: (n_ * nt + i, 0)
    const = lambda n_, i: (0, 0)
    if oa_transposed:
        oa_spec = pl.BlockSpec((None, aw, tm), lambda n_, i: (n_, 0, i))
    else:
        oa_spec = pl.BlockSpec((tm, aw), row)
    return pl.pallas_call(
        functools.partial(_merge_kernel, oa_transposed=oa_transposed),
        grid=(n, nt),
        in_specs=[
            oa_spec,
            pl.BlockSpec((tm, bw), row),
            pl.BlockSpec((tm, d), row),
            pl.BlockSpec((tm, d), row),
            pl.BlockSpec((tm, d), row),
            pl.BlockSpec((None, gt.shape[1], d), lambda n_, i: (n_, 0, 0)),
            pl.BlockSpec((1, d), const),
            pl.BlockSpec(wa.shape, const),
            pl.BlockSpec(wb.shape, const),
            pl.BlockSpec(wo.shape, const),
        ],
        out_specs=pl.BlockSpec((tm, d), row),
        out_shape=jax.ShapeDtypeStruct(x2.shape, f32),
        compiler_params=_params(("arbitrary", "arbitrary")),
        name="merge",
    )(oa, ob, ga, gb, x2, gt, gpm, wa, wb, wo)


def _ffn_kernel(x_ref, sc_ref, sh_ref, gt_ref, gpre_ref, gpost_ref, wi_ref, wo_ref, o_ref, *, ff, fc):
    x1 = x_ref[...]
    hb = _prenorm(x1, gpre_ref[...], sc_ref[...], sh_ref[...]).astype(bf16)
    f = jnp.zeros(x1.shape, f32)
    for c in range(ff // fc):
        gate = _dot(hb, wi_ref[:, c * fc:(c + 1) * fc])
        up = _dot(hb, wi_ref[:, ff + c * fc:ff + (c + 1) * fc])
        f = f + _dot((_silu(gate) * up).astype(bf16), wo_ref[c * fc:(c + 1) * fc, :])
    o_ref[...] = x1 + gt_ref[...] * (_rms(f) * gpost_ref[...])


def _ffn(x2, sc, sh, gt, gpre, gpost, wi, wo, n, t, tm):
    d = x2.shape[1]
    ff = wo.shape[0]
    fc = ff // 2 if (ff // 2) % LANES == 0 else ff
    nt = t // tm
    row = lambda n_, i: (n_ * nt + i, 0)
    const = lambda n_, i: (0, 0)
    mod = pl.BlockSpec((None, sc.shape[1], d), lambda n_, i: (n_, 0, 0))
    return pl.pallas_call(
        functools.partial(_ffn_kernel, ff=ff, fc=fc),
        grid=(n, nt),
        in_specs=[
            pl.BlockSpec((tm, d), row), mod, mod, mod,
            pl.BlockSpec((1, d), const),
            pl.BlockSpec((1, d), const),
            pl.BlockSpec(wi.shape, const),
            pl.BlockSpec(wo.shape, const),
        ],
        out_specs=pl.BlockSpec((tm, d), row),
        out_shape=jax.ShapeDtypeStruct(x2.shape, f32),
        compiler_params=_params(("arbitrary", "arbitrary")),
        name="ffn",
    )(x2, sc, sh, gt, gpre, gpost, wi, wo)


def kernel(x_prompt, x_sample, cache_k, cache_v, state_gdn, state_conv, page_table, c_prompt, c_sample,
           w_ada, b_ada, g_pre_mix, g_post_mix, g_pre_ffn, g_post_ffn, w_in, conv_w, a_log, dt_bias,
           gdn_norm_g, w_branch_a, w_branch_b, w_out, w_ffn_in, w_ffn_out):
    depth = w_in.shape[0]
    n, t, d = x_prompt.shape
    dec, dec_t, _ = x_sample.shape
    heads, hd = cache_k.shape[3], cache_k.shape[4]
    b_heads, dk = state_gdn.shape[2], state_gdn.shape[3]
    aw, bw = heads * hd, b_heads * dk
    conv_ch = conv_w.shape[2]
    taps = conv_w.shape[1]
    assert dec_t == 1 and heads == SUBLANES and 2 * hd == LANES and dk == LANES
    assert t % MOBA_BLOCK == 0 and MOBA_BLOCK % Q_BLOCK == 0 and dec % SUBLANES == 0 and taps == 4

    xp, xs = x_prompt, x_sample.reshape(dec, d)
    outs = {k: [] for k in ("kp", "vp", "ks", "vs", "sp", "ss", "bp", "bs")}
    rows = -(-(n + dec) // SUBLANES) * SUBLANES
    c_all = jnp.concatenate([c_prompt, c_sample, jnp.zeros((rows - n - dec, d), f32)], axis=0)

    for l in range(depth):
        mod = _ada(c_all, w_ada[l], b_ada[l])
        mods = [mod[:, j * d:(j + 1) * d] for j in range(N_MOD)]
        mp = [m[:n].reshape(n, 1, d) for m in mods]
        ms = [m[n:n + dec].reshape(1, dec, d) for m in mods]
        w, cols = _layout_w_in(w_in[l], aw, conv_ch, bw, b_heads, d)
        row = lambda a: a.reshape(1, -1)
        g1, g2, g3, g4 = row(g_pre_mix[l]), row(g_post_mix[l]), row(g_pre_ffn[l]), row(g_post_ffn[l])
        wa, wb, wo = w_branch_a[l].astype(bf16), w_branch_b[l].astype(bf16), w_out[l].astype(bf16)
        wi, wf = w_ffn_in[l].astype(bf16), w_ffn_out[l].astype(bf16)
        lane_pad = lambda a: jnp.zeros((1, LANES), f32).at[0, b_heads:2 * b_heads].set(a)
        pa, pb = lane_pad(a_log[l]), lane_pad(dt_bias[l])
        gn = row(gdn_norm_g[l])

        kt, vt, q16, k16, vt16, km, cb, zb, ga, gb, ba = _proj_prompt(xp, mp[1], mp[0], g1, w, cols, heads, hd,
                                                                      conv_ch, bw)
        oa_t = _moba_prompt(q16, k16, vt16, km, n, t, heads, hd)
        ob, s_fin = _gdn_prompt(cb, ba, zb, conv_w[l], pa, pb, gn, n, t, b_heads, dk)
        x2 = xp.reshape(n * t, d)
        x1 = _merge(oa_t, ob, ga, gb, x2, mp[2], g2, wa, wb, wo, n, t, MOBA_BLOCK, True)
        xp = _ffn(x1, mp[4], mp[3], mp[5], g3, g4, wi, wf, n, t, MOBA_BLOCK).reshape(n, t, d)
        outs["kp"].append(jnp.transpose(kt.reshape(n, heads, hd, t), (0, 3, 1, 2)))
        outs["vp"].append(jnp.transpose(vt.reshape(n, heads, hd, t), (0, 3, 1, 2)))
        outs["sp"].append(s_fin)
        outs["bp"].append(cb.reshape(n, t, conv_ch)[:, t - (taps - 1):])

        q_s, k_s, v_s, cb_s, zb_s, ga_s, gb_s, ba_s = _proj_sample(xs, ms[1][0], ms[0][0], g1, w, cols)
        oa_s = _moba_sample(q_s, k_s, v_s, cache_k, cache_v, page_table, l)
        bufs = [state_conv[l][:, j] for j in range(taps - 1)]
        ob_s, s_new = _gdn_sample(cb_s, bufs, ba_s, zb_s, state_gdn[l], conv_w[l], pa, pb, gn, b_heads, dk)
        x1s = _merge(oa_s, ob_s, ga_s, gb_s, xs, ms[2], g2, wa, wb, wo, 1, dec, dec, False)
        xs = _ffn(x1s, ms[4], ms[3], ms[5], g3, g4, wi, wf, 1, dec, dec)
        outs["ks"].append(k_s.reshape(dec, 1, heads, hd))
        outs["vs"].append(v_s.reshape(dec, 1, heads, hd))
        outs["ss"].append(s_new)
        outs["bs"].append(jnp.stack([bufs[1], bufs[2], cb_s], axis=1))

    st = lambda key: jnp.stack(outs[key])
    return (xp, xs.reshape(dec, 1, d), st("kp"), st("vp"), st("ks"), st("vs"), st("sp"), st("ss"), st("bp"), st("bs"))
```

```python
import functools

import jax
import jax.numpy as jnp
from jax import lax
from jax.experimental import pallas as pl
from jax.experimental.pallas import tpu as pltpu

f32 = jnp.float32
bf16 = jnp.bfloat16

EPS = 1e-6
MOBA_BLOCK = 256
MOBA_TOPK = 3
Q_BLOCK = 128
GDN_CHUNK = 64
GDN_STEP_TOKENS = 256
N_MOD = 6

LANES = 128
SUBLANES = 8
V_PAD_ROWS = 16
MOBA_DEPTH = 4
VMEM_LIMIT_BYTES = 56 * 1024 * 1024

HIGHEST = lax.Precision.HIGHEST
NEG_INF = float("-inf")
POS_INF = float("inf")
LOG2E = 1.4426950408889634
_TRANS_B = (((1,), (1,)), ((), ()))
_TRANS_A = (((0,), (0,)), ((), ()))


def _params(semantics):
    return pltpu.CompilerParams(dimension_semantics=semantics, vmem_limit_bytes=VMEM_LIMIT_BYTES)


def _dot(a, b):
    return jnp.dot(a, b, preferred_element_type=f32)


def _dot_nt(a, b):
    return lax.dot_general(a, b, _TRANS_B, preferred_element_type=f32)


def _dot_hi(a, b):
    return jnp.dot(a, b, precision=HIGHEST, preferred_element_type=f32)


def _split(x):
    hi = x.astype(bf16)
    return hi, (x - hi.astype(f32)).astype(bf16)


def _dot3(a, b):
    ah, al = _split(a)
    bh, bl = _split(b)
    return _dot(ah, bh) + (_dot(ah, bl) + _dot(al, bh))


def _dot_mask3(mask, x):
    x1 = x.astype(bf16)
    r1 = x - x1.astype(f32)
    x2 = r1.astype(bf16)
    x3 = (r1 - x2.astype(f32)).astype(bf16)
    return _dot(mask, x1) + (_dot(mask, x2) + _dot(mask, x3))


def _silu(x):
    return x * jax.nn.sigmoid(x)


def _softplus(x):
    return jnp.maximum(x, 0.0) + jnp.log1p(jnp.exp(-jnp.abs(x)))


def _rms(x):
    return x * lax.rsqrt(jnp.mean(x * x, axis=-1, keepdims=True) + EPS)


def _l2(x):
    return x * lax.rsqrt(jnp.sum(x * x, axis=-1, keepdims=True) + EPS)


def _alibi_slope(h, heads):
    return 2.0 ** (-8.0 * (h + 1) / heads)


def _ada_kernel(c_ref, w_ref, b_ref, o_ref):
    a = _silu(c_ref[...]).astype(bf16)
    o_ref[...] = _dot(a, w_ref[...].astype(bf16)) + b_ref[...]


def _ada(c_all, w_ada, b_ada):
    rows, d = c_all.shape
    nm = w_ada.shape[1]
    tn = 512
    return pl.pallas_call(
        _ada_kernel,
        grid=(nm // tn,),
        in_specs=[
            pl.BlockSpec((rows, d), lambda j: (0, 0)),
            pl.BlockSpec((d, tn), lambda j: (0, j)),
            pl.BlockSpec((1, tn), lambda j: (0, j)),
        ],
        out_specs=pl.BlockSpec((rows, tn), lambda j: (0, j)),
        out_shape=jax.ShapeDtypeStruct((rows, nm), f32),
        compiler_params=_params(("arbitrary",)),
        name="ada",
    )(c_all, w_ada, b_ada.reshape(1, nm))


def _prenorm(x, g, sc, sh):
    return (_rms(x) * g) * (1.0 + sc) + sh


def _proj_prompt_kernel(x_ref, sc_ref, sh_ref, g_ref, w_ref,
                        kt_ref, vt_ref, q16_ref, k16_ref, vt16_ref, km_ref, cb_ref, zb_ref, ga_ref, gb_ref, ba_ref,
                        *, hd, cols):
    hb = _prenorm(x_ref[...], g_ref[...], sc_ref[...], sh_ref[...]).astype(bf16)

    def proj(name):
        a, b = cols[name]
        return _dot(hb, w_ref[:, a:b])

    q, k, v = proj("q"), proj("k"), proj("v")
    tm = x_ref.shape[0]
    q16_ref[...] = (q * (hd ** -0.5 * LOG2E)).astype(bf16)
    km_ref[0] = jnp.mean(k, axis=0, keepdims=True)
    vrows = vt16_ref.shape[1]
    ones_row = (lax.broadcasted_iota(jnp.int32, (vrows - hd, tm), 0) == 0).astype(bf16)
    for g in range(k.shape[1] // LANES):
        cs = slice(g * LANES, (g + 1) * LANES)
        k16_ref[g] = k[:, cs].astype(bf16)
        kt_ref[cs, :] = k[:, cs].T
        vt = v[:, cs].T
        vt_ref[cs, :] = vt
        for e in range(LANES // hd):
            h = g * (LANES // hd) + e
            vt16_ref[h, 0:hd, :] = vt[e * hd:(e + 1) * hd].astype(bf16)
            vt16_ref[h, hd:vrows, :] = ones_row
    cb_ref[...] = proj("cb")
    zb_ref[...] = proj("zb")
    ga_ref[...] = proj("ga")
    gb_ref[...] = proj("gb")
    ba_ref[...] = proj("ba")


def _proj_sample_kernel(x_ref, sc_ref, sh_ref, g_ref, w_ref,
                        q_ref, k_ref, v_ref, cb_ref, zb_ref, ga_ref, gb_ref, ba_ref, *, cols):
    hb = _prenorm(x_ref[...], g_ref[...], sc_ref[...], sh_ref[...]).astype(bf16)
    for name, ref in (("q", q_ref), ("k", k_ref), ("v", v_ref), ("cb", cb_ref), ("zb", zb_ref),
                      ("ga", ga_ref), ("gb", gb_ref), ("ba", ba_ref)):
        a, b = cols[name]
        ref[...] = _dot(hb, w_ref[:, a:b])


def _layout_w_in(w_in, a_width, conv_ch, b_width, b_heads, d):
    o_q, o_k, o_v = 0, a_width, 2 * a_width
    o_cb = 3 * a_width
    o_zb = o_cb + conv_ch
    o_ba = o_zb + b_width
    o_ga = o_ba + 2 * b_heads
    o_gb = o_ga + d
    main = w_in[:, :o_ba]
    gates = w_in[:, o_ga:o_gb + d]
    ba = w_in[:, o_ba:o_ga]
    pad = jnp.zeros((w_in.shape[0], LANES - 2 * b_heads), w_in.dtype)
    w = jnp.concatenate([main, gates, ba, pad], axis=1).astype(bf16)
    cols = {"q": (o_q, o_k), "k": (o_k, o_v), "v": (o_v, o_cb), "cb": (o_cb, o_zb), "zb": (o_zb, o_ba),
            "ga": (o_ba, o_ba + d), "gb": (o_ba + d, o_ba + 2 * d), "ba": (o_ba + 2 * d, o_ba + 2 * d + LANES)}
    return w, cols


def _proj_prompt(x, sc, sh, g, w, cols, heads, hd, conv_ch, b_width):
    n, t, d = x.shape
    tm = MOBA_BLOCK
    nb = t // tm
    aw = heads * hd
    wcols = w.shape[1]
    row = lambda n_, i: (n_ * nb + i, 0)
    mod = lambda n_, i: (n_, 0, 0)
    tok_minor = lambda n_, i: (n_, 0, i)
    out_shape = (
        jax.ShapeDtypeStruct((n, aw, t), f32),
        jax.ShapeDtypeStruct((n, aw, t), f32),
        jax.ShapeDtypeStruct((n * t, aw), bf16),
        jax.ShapeDtypeStruct((n, nb, aw // LANES, tm, LANES), bf16),
        jax.ShapeDtypeStruct((n, nb, heads, hd + V_PAD_ROWS, tm), bf16),
        jax.ShapeDtypeStruct((n * nb, 1, aw), f32),
        jax.ShapeDtypeStruct((n * t, conv_ch), f32),
        jax.ShapeDtypeStruct((n * t, b_width), f32),
        jax.ShapeDtypeStruct((n * t, d), f32),
        jax.ShapeDtypeStruct((n * t, d), f32),
        jax.ShapeDtypeStruct((n * t, LANES), f32),
    )
    out_specs = (
        pl.BlockSpec((None, aw, tm), tok_minor),
        pl.BlockSpec((None, aw, tm), tok_minor),
        pl.BlockSpec((tm, aw), row),
        pl.BlockSpec((None, None, aw // LANES, tm, LANES), lambda n_, i: (n_, i, 0, 0, 0)),
        pl.BlockSpec((None, None, heads, hd + V_PAD_ROWS, tm), lambda n_, i: (n_, i, 0, 0, 0)),
        pl.BlockSpec((1, 1, aw), lambda n_, i: (n_ * nb + i, 0, 0)),
        pl.BlockSpec((tm, conv_ch), row),
        pl.BlockSpec((tm, b_width), row),
        pl.BlockSpec((tm, d), row),
        pl.BlockSpec((tm, d), row),
        pl.BlockSpec((tm, LANES), row),
    )
    return pl.pallas_call(
        functools.partial(_proj_prompt_kernel, hd=hd, cols=cols),
        grid=(n, nb),
        in_specs=[
            pl.BlockSpec((tm, d), row),
            pl.BlockSpec((None, 1, d), mod),
            pl.BlockSpec((None, 1, d), mod),
            pl.BlockSpec((1, d), lambda n_, i: (0, 0)),
            pl.BlockSpec((d, wcols), lambda n_, i: (0, 0)),
        ],
        out_specs=out_specs,
        out_shape=out_shape,
        compiler_params=_params(("arbitrary", "arbitrary")),
        name="proj_prompt",
    )(x.reshape(n * t, d), sc, sh, g, w)


def _proj_sample(x, sc, sh, g, w, cols):
    rows, d = x.shape
    wcols = w.shape[1]
    names = ("q", "k", "v", "cb", "zb", "ga", "gb", "ba")
    widths = [cols[nm][1] - cols[nm][0] for nm in names]
    full = lambda i: (0, 0)
    return pl.pallas_call(
        functools.partial(_proj_sample_kernel, cols=cols),
        grid=(1,),
        in_specs=[
            pl.BlockSpec((rows, d), full),
            pl.BlockSpec((rows, d), full),
            pl.BlockSpec((rows, d), full),
            pl.BlockSpec((1, d), full),
            pl.BlockSpec((d, wcols), full),
        ],
        out_specs=tuple(pl.BlockSpec((rows, wd), full) for wd in widths),
        out_shape=tuple(jax.ShapeDtypeStruct((rows, wd), f32) for wd in widths),
        compiler_params=_params(("arbitrary",)),
        name="proj_sample",
    )(x, sc, sh, g, w)


def _moba_prompt_kernel(q_ref, k_ref, vt_ref, kmh_ref, kml_ref, bias_ref, o_ref,
                        qq_ref, sel_ref, m_ref, acc_ref, s_ref, st_ref, *, heads, hd, topk):
    c = pl.program_id(1)
    qb = q_ref.shape[0]
    nbp = kmh_ref.shape[0]
    mb = k_ref.shape[2]
    per = LANES // hd
    pairs = heads // per
    own = (c * qb) // mb
    own_bias = 1 + (c * qb - own * mb) // qb
    lane = lax.broadcasted_iota(jnp.int32, (qb, LANES), 1)
    bid = lax.broadcasted_iota(jnp.int32, (nbp, qb), 0)

    for g in range(pairs):
        pair = slice(g * LANES, (g + 1) * LANES)
        qpair = q_ref[:, pair]
        for e in range(per):
            keep = jnp.logical_and(lane >= e * hd, lane < (e + 1) * hd)
            qq_ref[g, e * qb:(e + 1) * qb, :] = jnp.where(keep, qpair, jnp.zeros_like(qpair))
        qq = qq_ref[g]
        bs2 = _dot_nt(kmh_ref[:, pair], qq) + _dot_nt(kml_ref[:, pair], qq)
        for e in range(per):
            h = g * per + e
            bs = jnp.where(bid < own, bs2[:, e * qb:(e + 1) * qb], NEG_INF)
            sel = jnp.where(bid == own, 1.0, 0.0)
            for t in range(topk):
                mx = jnp.max(bs, axis=0, keepdims=True)
                idx = jnp.min(jnp.where(bs == mx, bid, nbp), axis=0, keepdims=True)
                hit = bid == idx
                sel = jnp.where(jnp.logical_and(hit, t < own), 1.0, sel)
                bs = jnp.where(hit, NEG_INF, bs)
            sel_ref[h] = sel
            m_ref[h] = jnp.full((1, qb), NEG_INF, f32)
            acc_ref[h] = jnp.zeros(acc_ref.shape[1:], f32)

    last = k_ref.shape[0] - 1

    def scores(i, slot):
        blk = jnp.where(i == 0, own, jnp.minimum(i - 1, last))
        tile = jnp.where(i == 0, own_bias, 0)
        live = i <= own
        for g in range(pairs):
            s2 = _dot_nt(k_ref[blk, g], qq_ref[g])
            for e in range(per):
                h = g * per + e
                on = jnp.logical_and(sel_ref[h, pl.ds(blk, 1), :] > 0.0, live)
                s = s2[:, e * qb:(e + 1) * qb] + bias_ref[tile, h]
                s_ref[slot, h] = s
                off = jnp.full((1, qb), blk - own, jnp.int32).astype(f32) * (_alibi_slope(h, heads) * mb * LOG2E)
                m = m_ref[h]
                m_new = jnp.where(on, jnp.maximum(m, jnp.max(s, axis=0, keepdims=True) + off), m)
                st_ref[slot, h] = jnp.concatenate([jnp.exp2(m - m_new), jnp.where(on, m_new - off, POS_INF)], axis=0)
                m_ref[h] = m_new

    def consume(i, slot):
        blk = jnp.where(i == 0, own, jnp.minimum(i - 1, last))
        for h in range(heads):
            st = st_ref[slot, h]
            p = jnp.exp2(s_ref[slot, h] - st[1:2])
            acc_ref[h] = st[0:1] * acc_ref[h] + _dot(vt_ref[blk, h], p.astype(bf16))

    depth = s_ref.shape[0] // 2
    for u in range(depth):
        scores(u, u)

    def body(i, carry):
        base = depth * lax.rem(i, 2)
        for u in range(depth):
            consume(depth * i + u, base + u)
        for u in range(depth):
            scores(depth * (i + 1) + u, depth - base + u)
        return carry

    trips = (own + depth) // depth
    lax.fori_loop(0, trips - 1, body, 0)
    final = trips - 1
    for u in range(depth):
        consume(depth * final + u, depth * lax.rem(final, 2) + u)
    for h in range(heads):
        acc = acc_ref[h]
        o_ref[h * hd:(h + 1) * hd, :] = (acc[:hd] / acc[hd:hd + 1]).astype(o_ref.dtype)


def _moba_prompt(q16, k16, vt16, km, n, t, heads, hd):
    aw = heads * hd
    nb = t // MOBA_BLOCK
    nq = t // Q_BLOCK
    per = LANES // hd
    pairs = heads // per
    vrows = vt16.shape[3]
    nbp = -(-nb // SUBLANES) * SUBLANES
    assert all(_alibi_slope(h, heads) * MOBA_BLOCK == int(_alibi_slope(h, heads) * MOBA_BLOCK) for h in range(heads))
    km = jnp.pad(km.reshape(n, nb, aw), ((0, 0), (0, nbp - nb), (0, 0)))
    km_hi = km.astype(bf16)
    km_lo = (km - km_hi.astype(f32)).astype(bf16)
    pos = jnp.arange(MOBA_BLOCK, dtype=f32)[None, :, None]
    slopes = jnp.asarray([_alibi_slope(h, heads) for h in range(heads)], f32)[:, None, None]
    plain = jnp.broadcast_to(slopes * pos * LOG2E, (heads, MOBA_BLOCK, Q_BLOCK))
    kpos = jnp.arange(MOBA_BLOCK)[:, None]
    qpos = jnp.arange(Q_BLOCK)[None, :]
    bias = jnp.stack([plain] + [jnp.where(kpos <= qpos + x * Q_BLOCK, plain, NEG_INF)
                                for x in range(MOBA_BLOCK // Q_BLOCK)])
    whole = lambda n_, c: (n_, 0, 0)
    resident = dict(pipeline_mode=pl.Buffered(1))
    return pl.pallas_call(
        functools.partial(_moba_prompt_kernel, heads=heads, hd=hd, topk=MOBA_TOPK),
        grid=(n, nq),
        in_specs=[
            pl.BlockSpec((None, Q_BLOCK, aw), lambda n_, c: (n_, c, 0)),
            pl.BlockSpec((None, nb, pairs, MOBA_BLOCK, LANES), lambda n_, c: (n_, 0, 0, 0, 0), **resident),
            pl.BlockSpec((None, nb, heads, vrows, MOBA_BLOCK), lambda n_, c: (n_, 0, 0, 0, 0), **resident),
            pl.BlockSpec((None, nbp, aw), whole),
            pl.BlockSpec((None, nbp, aw), whole),
            pl.BlockSpec((1 + MOBA_BLOCK // Q_BLOCK, heads, MOBA_BLOCK, Q_BLOCK), lambda n_, c: (0, 0, 0, 0)),
        ],
        out_specs=pl.BlockSpec((None, aw, Q_BLOCK), lambda n_, c: (n_, 0, c)),
        out_shape=jax.ShapeDtypeStruct((n, aw, t), bf16),
        scratch_shapes=[
            pltpu.VMEM((pairs, per * Q_BLOCK, LANES), bf16),
            pltpu.VMEM((heads, nbp, Q_BLOCK), f32),
            pltpu.VMEM((heads, 1, Q_BLOCK), f32),
            pltpu.VMEM((heads, vrows, Q_BLOCK), f32),
            pltpu.VMEM((2 * MOBA_DEPTH, heads, MOBA_BLOCK, Q_BLOCK), f32),
            pltpu.VMEM((2 * MOBA_DEPTH, heads, 2, Q_BLOCK), f32),
        ],
        compiler_params=_params(("arbitrary", "arbitrary")),
        name="moba_prompt",
    )(q16.reshape(n, t, aw), k16, vt16, km_hi, km_lo, bias)


def _sel_sample_kernel(pt_ref, q_ref, kn_ref, ck_ref, sel_ref, buf_ref, sem, *, layer, n_pages, ppb, topk, own):
    i = pl.program_id(0)
    slot = lax.rem(i, 2)

    def page_copy(sample, slot_, p):
        return pltpu.make_async_copy(ck_ref.at[layer, pt_ref[sample * n_pages + p]], buf_ref.at[slot_, p],
                                     sem.at[slot_])

    @pl.when(i == 0)
    def _():
        for p in range(n_pages):
            page_copy(0, 0, p).start()

    @pl.when(i + 1 < pl.num_programs(0))
    def _():
        for p in range(n_pages):
            page_copy(i + 1, 1 - slot, p).start()

    for p in range(n_pages):
        page_copy(i, slot, p).wait()

    heads, hd = q_ref.shape
    nbk = n_pages // ppb
    lane = lax.broadcasted_iota(jnp.int32, (hd, LANES), 1)
    q = q_ref[...]
    rows = []
    for h in range(heads):
        def add_block(b, m, h=h):
            s = buf_ref[slot, ppb * b, h]
            for j in range(1, ppb):
                s = s + buf_ref[slot, ppb * b + j, h]
            return jnp.where(lane == b, jnp.sum(s, axis=-1, keepdims=True), m)

        ksum_t = lax.fori_loop(0, nbk, add_block, jnp.zeros((hd, LANES), f32), unroll=True)
        rows.append(_dot_hi(q, ksum_t)[h:h + 1])
    lane8 = lax.broadcasted_iota(jnp.int32, (heads, LANES), 1)
    bs = jnp.concatenate(rows, axis=0) / MOBA_BLOCK
    sc_new = jnp.sum(kn_ref[...] * q, axis=-1, keepdims=True) / MOBA_BLOCK
    bs = jnp.where(lane8 == own, sc_new, bs)
    bs = jnp.where(lane8 < own, bs, NEG_INF)
    out = jnp.zeros((heads, LANES), jnp.int32)
    for t in range(topk):
        mx = jnp.max(bs, axis=-1, keepdims=True)
        idx = jnp.min(jnp.where(bs == mx, lane8, LANES), axis=-1, keepdims=True)
        out = jnp.where(lane8 == t, idx, out)
        bs = jnp.where(lane8 == idx, NEG_INF, bs)
    sel_ref[...] = out


def _attn_sample_kernel(pt_ref, sel_ref, q_ref, kn_ref, vn_ref, ck_ref, cv_ref, o_ref, kbuf, vbuf, sem,
                        *, layer, n_pages, ppb, topk, own, past):
    i = pl.program_id(0)
    slot = lax.rem(i, 2)
    heads, hd = q_ref.shape
    page = kbuf.shape[-1]

    def slab_copies(sample, slot_):
        out = []
        for h in range(heads):
            for t in range(topk):
                blk = sel_ref[(sample * heads + h) * topk + t]
                for j in range(ppb):
                    pg = pt_ref[sample * n_pages + ppb * blk + j]
                    r = (h * topk + t) * ppb + j
                    out.append(pltpu.make_async_copy(ck_ref.at[layer, pg, h], kbuf.at[slot_, r], sem.at[0, slot_]))
                    out.append(pltpu.make_async_copy(cv_ref.at[layer, pg, h], vbuf.at[slot_, r], sem.at[1, slot_]))
        return out

    @pl.when(i == 0)
    def _():
        for cp in slab_copies(0, 0):
            cp.start()

    @pl.when(i + 1 < pl.num_programs(0))
    def _():
        for cp in slab_copies(i + 1, 1 - slot):
            cp.start()

    for cp in slab_copies(i, slot):
        cp.wait()

    scale = hd ** -0.5
    q = q_ref[...]
    qb = (q * scale).astype(bf16)
    lane = lax.broadcasted_iota(jnp.int32, (1, page), 1)
    for h in range(heads):
        slope = _alibi_slope(h, heads)
        pieces = []
        for t in range(topk):
            blk = sel_ref[(i * heads + h) * topk + t]
            for j in range(ppb):
                r = (h * topk + t) * ppb + j
                if t < own:
                    s = _dot(qb, kbuf[slot, r].astype(bf16))[h:h + 1]
                    pos = blk * MOBA_BLOCK + j * page + lane
                    pieces.append(s - slope * (past - pos).astype(f32))
                else:
                    pieces.append(jnp.full((1, page), NEG_INF, f32))
        s_sel = jnp.concatenate(pieces, axis=1)
        s_own = jnp.sum(q[h:h + 1] * kn_ref[h:h + 1, :], axis=-1, keepdims=True) * scale
        m = jnp.maximum(jnp.max(s_sel, axis=-1, keepdims=True), s_own)
        p = jnp.exp(s_sel - m)
        p_own = jnp.exp(s_own - m)
        den = jnp.sum(p, axis=-1, keepdims=True) + p_own
        num = p_own * vn_ref[h:h + 1, :]
        for x in range(topk * ppb):
            px = jnp.broadcast_to(p[:, x * page:(x + 1) * page], (SUBLANES, page)).astype(bf16)
            num = num + _dot_nt(px, vbuf[slot, h * topk * ppb + x].astype(bf16))[0:1]
        o_ref[h:h + 1, :] = num / den


def _moba_sample(q_s, k_s, v_s, cache_k, cache_v, page_table, layer):
    dec = q_s.shape[0]
    _, n_pool, page, heads, hd = cache_k.shape
    n_pages = page_table.shape[1]
    past = n_pages * page
    ppb = MOBA_BLOCK // page
    assert MOBA_BLOCK % page == 0 and past % MOBA_BLOCK == 0 and page == LANES
    own = past // MOBA_BLOCK
    assert own < LANES
    topk = min(MOBA_TOPK, own + 1)
    q3 = q_s.reshape(dec, heads, hd)
    kn3 = k_s.reshape(dec, heads, hd)
    vn3 = v_s.reshape(dec, heads, hd)
    ck = jnp.transpose(cache_k, (0, 1, 3, 4, 2))
    cv = jnp.transpose(cache_v, (0, 1, 3, 4, 2))
    pt_flat = page_table.reshape(-1).astype(jnp.int32)
    per_sample = lambda i, *_: (i, 0, 0)
    hbm = pl.BlockSpec(memory_space=pl.ANY)

    sel = pl.pallas_call(
        functools.partial(_sel_sample_kernel, layer=layer, n_pages=n_pages, ppb=ppb, topk=topk, own=own),
        grid_spec=pltpu.PrefetchScalarGridSpec(
            num_scalar_prefetch=1,
            grid=(dec,),
            in_specs=[pl.BlockSpec((None, heads, hd), per_sample), pl.BlockSpec((None, heads, hd), per_sample), hbm],
            out_specs=pl.BlockSpec((None, heads, LANES), per_sample),
            scratch_shapes=[pltpu.VMEM((2, n_pages, heads, hd, page), f32), pltpu.SemaphoreType.DMA((2,))],
        ),
        out_shape=jax.ShapeDtypeStruct((dec, heads, LANES), jnp.int32),
        compiler_params=_params(("arbitrary",)),
        name="moba_sample_select",
    )(pt_flat, q3, kn3, ck)
    sel_flat = sel[:, :, :topk].reshape(-1)

    n_slabs = heads * topk * ppb
    o3 = pl.pallas_call(
        functools.partial(_attn_sample_kernel, layer=layer, n_pages=n_pages, ppb=ppb, topk=topk, own=own, past=past),
        grid_spec=pltpu.PrefetchScalarGridSpec(
            num_scalar_prefetch=2,
            grid=(dec,),
            in_specs=[pl.BlockSpec((None, heads, hd), per_sample), pl.BlockSpec((None, heads, hd), per_sample),
                      pl.BlockSpec((None, heads, hd), per_sample), hbm, hbm],
            out_specs=pl.BlockSpec((None, heads, hd), per_sample),
            scratch_shapes=[pltpu.VMEM((2, n_slabs, hd, page), f32), pltpu.VMEM((2, n_slabs, hd, page), f32),
                            pltpu.SemaphoreType.DMA((2, 2))],
        ),
        out_shape=jax.ShapeDtypeStruct((dec, heads, hd), f32),
        compiler_params=_params(("arbitrary",)),
        name="moba_sample_attend",
    )(pt_flat, sel_flat, q3, kn3, vn3, ck, cv)
    return o3.reshape(dec, heads * hd)


def _gdn_masks(td, ch):
    ri = lax.broadcasted_iota(jnp.int32, (td, td), 0)
    ci = lax.broadcasted_iota(jnp.int32, (td, td), 1)
    same = ri // ch == ci // ch
    masks = [jnp.logical_and(same, ri >= ci), jnp.logical_and(same, ri > ci)]
    b = 1
    while b < ch:
        masks.append(jnp.logical_and(jnp.logical_and(ri // (2 * b) == ci // (2 * b), ri % (2 * b) >= b),
                                     ci % (2 * b) < b))
        b *= 2
    return [m.astype(f32) for m in masks]


def _bdot(a, b):
    return jnp.einsum("bij,bjk->bik", a, b, preferred_element_type=f32)


def _bdot_nt(a, b):
    return jnp.einsum("bik,bjk->bij", a, b, preferred_element_type=f32)


def _bdot_tn(a, b):
    return jnp.einsum("bki,bkj->bij", a, b, preferred_element_type=f32)


def _bdot3(a, b):
    ah, al = _split(a)
    bh, bl = _split(b)
    return _bdot(ah, bh) + (_bdot(ah, bl) + _bdot(al, bh))


def _gdn_prompt_kernel(cb_ref, ba_ref, zb_ref, cw_ref, pa_ref, pb_ref, gn_ref, o_ref, sfin_ref,
                       xp_ref, s_ref, mk_ref, *, heads, dk, ch, taps):
    nseq, tt = cb_ref.shape[0], cb_ref.shape[1]
    td = mk_ref.shape[1]
    nt = tt // td
    bw = heads * dk
    pad = SUBLANES
    nlev = mk_ref.shape[0] - 2

    @pl.when(pl.program_id(0) == 0)
    def _():
        for x, m in enumerate(_gdn_masks(td, ch)):
            mk_ref[x] = m
        xp_ref[:, 0:pad, :] = jnp.zeros((nseq, pad, xp_ref.shape[2]), f32)
        s_ref[...] = jnp.zeros_like(s_ref)

    w = cw_ref[...]
    gn = gn_ref[...]
    chains = [(b, h, x) for b in range(nseq) for h in range(heads) for x in range(nt)]
    q_l, k_l, v_l, beta_l, g_l, z_l = [], [], [], [], [], []
    for b in range(nseq):
        xp_ref[b, pad:pad + tt, :] = cb_ref[b]
        y = w[0:1] * xp_ref[b, pad - (taps - 1):pad - (taps - 1) + tt, :]
        for j in range(1, taps):
            y = y + w[j:j + 1] * xp_ref[b, pad - (taps - 1) + j:pad - (taps - 1) + j + tt, :]
        xp_ref[b, 0:pad, :] = xp_ref[b, tt:tt + pad, :]
        c = _silu(y)
        ba = ba_ref[b]
        beta_all = jax.nn.sigmoid(ba)
        g_all = -jnp.exp(pa_ref[...]) * _softplus(ba + pb_ref[...])
        for h in range(heads):
            for x in range(nt):
                rows = slice(x * td, (x + 1) * td)
                q_l.append(c[rows, h * dk:(h + 1) * dk])
                k_l.append(c[rows, bw + h * dk:bw + (h + 1) * dk])
                v_l.append(c[rows, 2 * bw + h * dk:2 * bw + (h + 1) * dk])
                beta_l.append(jnp.broadcast_to(beta_all[rows, h:h + 1], (td, dk)))
                g_l.append(jnp.broadcast_to(g_all[rows, heads + h:heads + h + 1], (td, dk)))
                z_l.append(zb_ref[b, rows, h * dk:(h + 1) * dk])
    nc = len(chains)
    qn = _l2(jnp.stack(q_l)) * (dk ** -0.5)
    kn = _l2(jnp.stack(k_l))
    vh = jnp.stack(v_l)
    beta_b = jnp.stack(beta_l)
    incl, strict = mk_ref[0], mk_ref[1]
    gc2 = _dot_mask3(incl.astype(bf16), jnp.concatenate(g_l, axis=1))
    gcum = jnp.stack([gc2[:, x * dk:(x + 1) * dk] for x in range(nc)])
    gcol = jnp.concatenate([gcum] * (td // dk), axis=2) if td != dk else gcum
    diff = (gcol - jnp.swapaxes(gcol, 1, 2)) * incl
    decay = jnp.exp(diff) * incl
    eg = jnp.exp(gcum)
    kb = kn * beta_b
    kb16, kn16 = kb.astype(bf16), kn.astype(bf16)
    a = _bdot_nt(kb16, kn16) * decay * strict
    e = -(a * mk_ref[2])
    for lv in range(1, nlev):
        r = a * mk_ref[2 + lv]
        y2 = r + _bdot(e.astype(bf16), r.astype(bf16))
        e = e - y2 - _bdot(y2.astype(bf16), e.astype(bf16))
    rhs = jnp.concatenate([vh * beta_b, kb * eg], axis=2)
    uw = rhs + _bdot3(e, rhs)
    u, wv16 = uw[:, :, :dk], uw[:, :, dk:].astype(bf16)
    qk = _bdot_nt(qn.astype(bf16), kn16) * decay
    g_end = jnp.concatenate([jnp.broadcast_to(gcum[:, (x + 1) * ch - 1:(x + 1) * ch, :], (nc, ch, dk))
                             for x in range(td // ch)], axis=1)
    k_dec = (kn * jnp.exp(g_end - gcum)).astype(bf16)
    q_dec = (qn * eg).astype(bf16)
    g_last = jnp.exp(g_end)

    ns = nseq * heads
    pick = lambda arr, x: arr.reshape((ns, nt) + arr.shape[1:])[:, x]
    s = s_ref[...].reshape(ns, dk, dk)
    v_new = [[None] * (td // ch) for _ in range(nt)]
    o_int = [[None] * (td // ch) for _ in range(nt)]
    for x in range(nt):
        u_x, w_x, qd_x, kd_x, gl_x = pick(u, x), pick(wv16, x), pick(q_dec, x), pick(k_dec, x), pick(g_last, x)
        for cix in range(td // ch):
            sl = slice(cix * ch, (cix + 1) * ch)
            sb = s.astype(bf16)
            vn = u_x[:, sl] - _bdot(w_x[:, sl], sb)
            o_int[x][cix] = _bdot(qd_x[:, sl], sb)
            s = s * gl_x[:, (cix + 1) * ch - 1:(cix + 1) * ch, :] + _bdot_tn(kd_x[:, sl], vn.astype(bf16))
            v_new[x][cix] = vn
    s_ref[...] = s.reshape(nseq, heads, dk, dk)
    sfin_ref[...] = s.reshape(nseq, heads, dk, dk)
    vn_all = jnp.stack([jnp.concatenate(v_new[x], axis=1) for x in range(nt)], axis=1).reshape(nc, td, dk)
    oi_all = jnp.stack([jnp.concatenate(o_int[x], axis=1) for x in range(nt)], axis=1).reshape(nc, td, dk)
    o = oi_all + _bdot(qk.astype(bf16), vn_all.astype(bf16))
    on = (_rms(o) * gn * _silu(jnp.stack(z_l))).astype(o_ref.dtype)
    for i, (b, h, x) in enumerate(chains):
        o_ref[b, x * td:(x + 1) * td, h * dk:(h + 1) * dk] = on[i]


def _gdn_prompt(cb, ba, zb, conv_w, pa, pb, gn, n, t, heads, dk):
    td = dk
    tt = GDN_STEP_TOKENS if t % GDN_STEP_TOKENS == 0 else td
    conv_ch = cb.shape[1]
    bw = heads * dk
    nt = t // tt
    ch = GDN_CHUNK
    assert t % tt == 0 and tt % td == 0 and td % ch == 0 and ch & (ch - 1) == 0
    nmask = 2 + ch.bit_length() - 1
    tile = lambda i: (0, i, 0)
    const = lambda i: (0, 0)
    o, s_fin = pl.pallas_call(
        functools.partial(_gdn_prompt_kernel, heads=heads, dk=dk, ch=ch, taps=conv_w.shape[0]),
        grid=(nt,),
        in_specs=[
            pl.BlockSpec((n, tt, conv_ch), tile),
            pl.BlockSpec((n, tt, LANES), tile),
            pl.BlockSpec((n, tt, bw), tile),
            pl.BlockSpec(conv_w.shape, const),
            pl.BlockSpec((1, LANES), const),
            pl.BlockSpec((1, LANES), const),
            pl.BlockSpec((1, dk), const),
        ],
        out_specs=(
            pl.BlockSpec((n, tt, bw), tile),
            pl.BlockSpec((n, heads, dk, dk), lambda i: (0, 0, 0, 0)),
        ),
        out_shape=(
            jax.ShapeDtypeStruct((n, t, bw), bf16),
            jax.ShapeDtypeStruct((n, heads, dk, dk), f32),
        ),
        scratch_shapes=[pltpu.VMEM((n, tt + SUBLANES, conv_ch), f32), pltpu.VMEM((n, heads, dk, dk), f32),
                        pltpu.VMEM((nmask, td, td), f32)],
        compiler_params=_params(("arbitrary",)),
        name="gdn_prompt",
    )(cb.reshape(n, t, conv_ch), ba.reshape(n, t, LANES), zb.reshape(n, t, bw), conv_w, pa, pb, gn)
    return o.reshape(n * t, bw), s_fin


def _gdn_sample_kernel(cb_ref, b0_ref, b1_ref, b2_ref, ba_ref, zb_ref, s_ref, cw_ref, pa_ref, pb_ref, gn_ref,
                       o_ref, so_ref, *, heads, dk):
    sb = cb_ref.shape[0]
    bw = heads * dk
    w = cw_ref[...]
    y = w[0:1] * b0_ref[...]
    y = y + w[1:2] * b1_ref[...]
    y = y + w[2:3] * b2_ref[...]
    y = y + w[3:4] * cb_ref[...]
    c = _silu(y)
    ba = ba_ref[...]
    beta_all = jax.nn.sigmoid(ba)
    g_all = -jnp.exp(pa_ref[...]) * _softplus(ba + pb_ref[...])
    ri = lax.broadcasted_iota(jnp.int32, (dk, dk), 0)
    ci = lax.broadcasted_iota(jnp.int32, (dk, dk), 1)
    ones = jnp.ones((dk, dk), f32)
    gn = gn_ref[...]
    zrows = jnp.zeros((SUBLANES - 2, dk), f32)
    for h in range(heads):
        qn = _l2(c[:, h * dk:(h + 1) * dk]) * (dk ** -0.5)
        kn = _l2(c[:, bw + h * dk:bw + (h + 1) * dk])
        vh = c[:, 2 * bw + h * dk:2 * bw + (h + 1) * dk]
        beta_b = jnp.broadcast_to(beta_all[:, h:h + 1], (sb, dk))
        eg = jnp.exp(jnp.broadcast_to(g_all[:, heads + h:heads + h + 1], (sb, dk)))
        zh = zb_ref[:, h * dk:(h + 1) * dk]
        qk = jnp.sum(qn * kn, axis=-1, keepdims=True)
        for j in range(sb):
            r = slice(j, j + 1)
            s = s_ref[j, h]
            lhs = jnp.concatenate([kn[r] * beta_b[r] * eg[r], qn[r] * eg[r], zrows], axis=0)
            rs = _dot_hi(lhs, s)
            v_new = vh[r] * beta_b[r] - rs[0:1]
            o = rs[1:2] + qk[r] * v_new
            kcol = _dot_hi(jnp.where(ri == ci, jnp.broadcast_to(kn[r], (dk, dk)), 0.0), ones)
            so_ref[j, h] = s * eg[r] + kcol * v_new
            on = _rms(o) * gn * _silu(zh[r])
            o_ref[r, h * dk:(h + 1) * dk] = on.astype(o_ref.dtype)


def _gdn_sample(cb, bufs, ba, zb, state, conv_w, pa, pb, gn, heads, dk):
    dec, conv_ch = cb.shape
    bw = heads * dk
    sb = SUBLANES
    row = lambda i: (i, 0)
    const = lambda i: (0, 0)
    st = lambda i: (i, 0, 0, 0)
    return pl.pallas_call(
        functools.partial(_gdn_sample_kernel, heads=heads, dk=dk),
        grid=(dec // sb,),
        in_specs=[
            pl.BlockSpec((sb, conv_ch), row),
            pl.BlockSpec((sb, conv_ch), row),
            pl.BlockSpec((sb, conv_ch), row),
            pl.BlockSpec((sb, conv_ch), row),
            pl.BlockSpec((sb, LANES), row),
            pl.BlockSpec((sb, bw), row),
            pl.BlockSpec((sb, heads, dk, dk), st),
            pl.BlockSpec(conv_w.shape, const),
            pl.BlockSpec((1, LANES), const),
            pl.BlockSpec((1, LANES), const),
            pl.BlockSpec((1, dk), const),
        ],
        out_specs=(pl.BlockSpec((sb, bw), row), pl.BlockSpec((sb, heads, dk, dk), st)),
        out_shape=(jax.ShapeDtypeStruct((dec, bw), bf16), jax.ShapeDtypeStruct(state.shape, f32)),
        compiler_params=_params(("arbitrary",)),
        name="gdn_sample",
    )(cb, bufs[0], bufs[1], bufs[2], ba, zb, state, conv_w, pa, pb, gn)


def _merge_kernel(oa_ref, ob_ref, ga_ref, gb_ref, x_ref, gt_ref, gpm_ref, wa_ref, wb_ref, wo_ref, o_ref,
                  *, oa_transposed):
    if oa_transposed:
        ya = lax.dot_general(oa_ref[...], wa_ref[...], _TRANS_A, preferred_element_type=f32)
    else:
        ya = _dot(oa_ref[...].astype(bf16), wa_ref[...])
    yb = _dot(ob_ref[...], wb_ref[...])
    y = jax.nn.sigmoid(ga_ref[...]) * ya + jax.nn.sigmoid(gb_ref[...]) * yb
    y2 = _dot(y.astype(bf16), wo_ref[...])
    o_ref[...] = x_ref[...] + gt_ref[...] * (_rms(y2) * gpm_ref[...])


def _merge(oa, ob, ga, gb, x2, gt, gpm, wa, wb, wo, n, t, tm, oa_transposed):
    d = x2.shape[1]
    aw = wa.shape[0]
    bw = wb.shape[0]
    nt = t // tm
    row = lambda n_, i: (n_ * nt + i, 0)
    const = lambda n_, i: (0, 0)
    if oa_transposed:
        oa_spec = pl.BlockSpec((None, aw, tm), lambda n_, i: (n_, 0, i))
    else:
        oa_spec = pl.BlockSpec((tm, aw), row)
    return pl.pallas_call(
        functools.partial(_merge_kernel, oa_transposed=oa_transposed),
        grid=(n, nt),
        in_specs=[
            oa_spec,
            pl.BlockSpec((tm, bw), row),
            pl.BlockSpec((tm, d), row),
            pl.BlockSpec((tm, d), row),
            pl.BlockSpec((tm, d), row),
            pl.BlockSpec((None, gt.shape[1], d), lambda n_, i: (n_, 0, 0)),
            pl.BlockSpec((1, d), const),
            pl.BlockSpec(wa.shape, const),
            pl.BlockSpec(wb.shape, const),
            pl.BlockSpec(wo.shape, const),
        ],
        out_specs=pl.BlockSpec((tm, d), row),
        out_shape=jax.ShapeDtypeStruct(x2.shape, f32),
        compiler_params=_params(("arbitrary", "arbitrary")),
        name="merge",
    )(oa, ob, ga, gb, x2, gt, gpm, wa, wb, wo)


def _ffn_kernel(x_ref, sc_ref, sh_ref, gt_ref, gpre_ref, gpost_ref, wi_ref, wo_ref, o_ref, *, ff, fc):
    x1 = x_ref[...]
    hb = _prenorm(x1, gpre_ref[...], sc_ref[...], sh_ref[...]).astype(bf16)
    f = jnp.zeros(x1.shape, f32)
    for c in range(ff // fc):
        gate = _dot(hb, wi_ref[:, c * fc:(c + 1) * fc])
        up = _dot(hb, wi_ref[:, ff + c * fc:ff + (c + 1) * fc])
        f = f + _dot((_silu(gate) * up).astype(bf16), wo_ref[c * fc:(c + 1) * fc, :])
    o_ref[...] = x1 + gt_ref[...] * (_rms(f) * gpost_ref[...])


def _ffn(x2, sc, sh, gt, gpre, gpost, wi, wo, n, t, tm):
    d = x2.shape[1]
    ff = wo.shape[0]
    fc = ff // 2 if (ff // 2) % LANES == 0 else ff
    nt = t // tm
    row = lambda n_, i: (n_ * nt + i, 0)
    const = lambda n_, i: (0, 0)
    mod = pl.BlockSpec((None, sc.shape[1], d), lambda n_, i: (n_, 0, 0))
    return pl.pallas_call(
        functools.partial(_ffn_kernel, ff=ff, fc=fc),
        grid=(n, nt),
        in_specs=[
            pl.BlockSpec((tm, d), row), mod, mod, mod,
            pl.BlockSpec((1, d), const),
            pl.BlockSpec((1, d), const),
            pl.BlockSpec(wi.shape, const),
            pl.BlockSpec(wo.shape, const),
        ],
        out_specs=pl.BlockSpec((tm, d), row),
        out_shape=jax.ShapeDtypeStruct(x2.shape, f32),
        compiler_params=_params(("arbitrary", "arbitrary")),
        name="ffn",
    )(x2, sc, sh, gt, gpre, gpost, wi, wo)


def kernel(x_prompt, x_sample, cache_k, cache_v, state_gdn, state_conv, page_table, c_prompt, c_sample,
           w_ada, b_ada, g_pre_mix, g_post_mix, g_pre_ffn, g_post_ffn, w_in, conv_w, a_log, dt_bias,
           gdn_norm_g, w_branch_a, w_branch_b, w_out, w_ffn_in, w_ffn_out):
    depth = w_in.shape[0]
    n, t, d = x_prompt.shape
    dec, dec_t, _ = x_sample.shape
    heads, hd = cache_k.shape[3], cache_k.shape[4]
    b_heads, dk = state_gdn.shape[2], state_gdn.shape[3]
    aw, bw = heads * hd, b_heads * dk
    conv_ch = conv_w.shape[2]
    taps = conv_w.shape[1]
    assert dec_t == 1 and heads == SUBLANES and 2 * hd == LANES and dk == LANES
    assert t % MOBA_BLOCK == 0 and MOBA_BLOCK % Q_BLOCK == 0 and dec % SUBLANES == 0 and taps == 4

    xp, xs = x_prompt, x_sample.reshape(dec, d)
    outs = {k: [] for k in ("kp", "vp", "ks", "vs", "sp", "ss", "bp", "bs")}
    rows = -(-(n + dec) // SUBLANES) * SUBLANES
    c_all = jnp.concatenate([c_prompt, c_sample, jnp.zeros((rows - n - dec, d), f32)], axis=0)

    for l in range(depth):
        mod = _ada(c_all, w_ada[l], b_ada[l])
        mods = [mod[:, j * d:(j + 1) * d] for j in range(N_MOD)]
        mp = [m[:n].reshape(n, 1, d) for m in mods]
        ms = [m[n:n + dec].reshape(1, dec, d) for m in mods]
        w, cols = _layout_w_in(w_in[l], aw, conv_ch, bw, b_heads, d)
        row = lambda a: a.reshape(1, -1)
        g1, g2, g3, g4 = row(g_pre_mix[l]), row(g_post_mix[l]), row(g_pre_ffn[l]), row(g_post_ffn[l])
        wa, wb, wo = w_branch_a[l].astype(bf16), w_branch_b[l].astype(bf16), w_out[l].astype(bf16)
        wi, wf = w_ffn_in[l].astype(bf16), w_ffn_out[l].astype(bf16)
        lane_pad = lambda a: jnp.zeros((1, LANES), f32).at[0, b_heads:2 * b_heads].set(a)
        pa, pb = lane_pad(a_log[l]), lane_pad(dt_bias[l])
        gn = row(gdn_norm_g[l])

        kt, vt, q16, k16, vt16, km, cb, zb, ga, gb, ba = _proj_prompt(xp, mp[1], mp[0], g1, w, cols, heads, hd,
                                                                      conv_ch, bw)
        oa_t = _moba_prompt(q16, k16, vt16, km, n, t, heads, hd)
        ob, s_fin = _gdn_prompt(cb, ba, zb, conv_w[l], pa, pb, gn, n, t, b_heads, dk)
        x2 = xp.reshape(n * t, d)
        x1 = _merge(oa_t, ob, ga, gb, x2, mp[2], g2, wa, wb, wo, n, t, MOBA_BLOCK, True)
        xp = _ffn(x1, mp[4], mp[3], mp[5], g3, g4, wi, wf, n, t, MOBA_BLOCK).reshape(n, t, d)
        outs["kp"].append(jnp.transpose(kt.reshape(n, heads, hd, t), (0, 3, 1, 2)))
        outs["vp"].append(jnp.transpose(vt.reshape(n, heads, hd, t), (0, 3, 1, 2)))
        outs["sp"].append(s_fin)
        outs["bp"].append(cb.reshape(n, t, conv_ch)[:, t - (taps - 1):])

        q_s, k_s, v_s, cb_s, zb_s, ga_s, gb_s, ba_s = _proj_sample(xs, ms[1][0], ms[0][0], g1, w, cols)
        oa_s = _moba_sample(q_s, k_s, v_s, cache_k, cache_v, page_table, l)
        bufs = [state_conv[l][:, j] for j in range(taps - 1)]
        ob_s, s_new = _gdn_sample(cb_s, bufs, ba_s, zb_s, state_gdn[l], conv_w[l], pa, pb, gn, b_heads, dk)
        x1s = _merge(oa_s, ob_s, ga_s, gb_s, xs, ms[2], g2, wa, wb, wo, 1, dec, dec, False)
        xs = _ffn(x1s, ms[4], ms[3], ms[5], g3, g4, wi, wf, 1, dec, dec)
        outs["ks"].append(k_s.reshape(dec, 1, heads, hd))
        outs["vs"].append(v_s.reshape(dec, 1, heads, hd))
        outs["ss"].append(s_new)
        outs["bs"].append(jnp.stack([bufs[1], bufs[2], cb_s], axis=1))

    st = lambda key: jnp.stack(outs[key])
    return (xp, xs.reshape(dec, 1, d), st("kp"), st("vp"), st("ks"), st("vs"), st("sp"), st("ss"), st("bp"), st("bs"))
```

```python
import functools

import jax
import jax.numpy as jnp
from jax import lax
from jax.experimental import pallas as pl
from jax.experimental.pallas import tpu as pltpu

f32 = jnp.float32
bf16 = jnp.bfloat16

EPS = 1e-6
MOBA_BLOCK = 256
MOBA_TOPK = 3
Q_BLOCK = 128
GDN_CHUNK = 64
GDN_STEP_TOKENS = 256
N_MOD = 6

LANES = 128
SUBLANES = 8
V_PAD_ROWS = 16
MOBA_DEPTH = 4
VMEM_LIMIT_BYTES = 56 * 1024 * 1024

HIGHEST = lax.Precision.HIGHEST
NEG_INF = float("-inf")
POS_INF = float("inf")
LOG2E = 1.4426950408889634
_TRANS_B = (((1,), (1,)), ((), ()))
_TRANS_A = (((0,), (0,)), ((), ()))


def _params(semantics):
    return pltpu.CompilerParams(dimension_semantics=semantics, vmem_limit_bytes=VMEM_LIMIT_BYTES)


def _dot(a, b):
    return jnp.dot(a, b, preferred_element_type=f32)


def _dot_nt(a, b):
    return lax.dot_general(a, b, _TRANS_B, preferred_element_type=f32)


def _dot_hi(a, b):
    return jnp.dot(a, b, precision=HIGHEST, preferred_element_type=f32)


def _split(x):
    hi = x.astype(bf16)
    return hi, (x - hi.astype(f32)).astype(bf16)


def _dot3(a, b):
    ah, al = _split(a)
    bh, bl = _split(b)
    return _dot(ah, bh) + (_dot(ah, bl) + _dot(al, bh))


def _dot_mask3(mask, x):
    x1 = x.astype(bf16)
    r1 = x - x1.astype(f32)
    x2 = r1.astype(bf16)
    x3 = (r1 - x2.astype(f32)).astype(bf16)
    return _dot(mask, x1) + (_dot(mask, x2) + _dot(mask, x3))


def _silu(x):
    return x * jax.nn.sigmoid(x)


def _softplus(x):
    return jnp.maximum(x, 0.0) + jnp.log1p(jnp.exp(-jnp.abs(x)))


def _rms(x):
    return x * lax.rsqrt(jnp.mean(x * x, axis=-1, keepdims=True) + EPS)


def _l2(x):
    return x * lax.rsqrt(jnp.sum(x * x, axis=-1, keepdims=True) + EPS)


def _alibi_slope(h, heads):
    return 2.0 ** (-8.0 * (h + 1) / heads)


def _ada_kernel(c_ref, w_ref, b_ref, o_ref):
    a = _silu(c_ref[...]).astype(bf16)
    o_ref[...] = _dot(a, w_ref[...].astype(bf16)) + b_ref[...]


def _ada(c_all, w_ada, b_ada):
    rows, d = c_all.shape
    nm = w_ada.shape[1]
    tn = 512
    return pl.pallas_call(
        _ada_kernel,
        grid=(nm // tn,),
        in_specs=[
            pl.BlockSpec((rows, d), lambda j: (0, 0)),
            pl.BlockSpec((d, tn), lambda j: (0, j)),
            pl.BlockSpec((1, tn), lambda j: (0, j)),
        ],
        out_specs=pl.BlockSpec((rows, tn), lambda j: (0, j)),
        out_shape=jax.ShapeDtypeStruct((rows, nm), f32),
        compiler_params=_params(("arbitrary",)),
        name="ada",
    )(c_all, w_ada, b_ada.reshape(1, nm))


def _prenorm(x, g, sc, sh):
    return (_rms(x) * g) * (1.0 + sc) + sh


def _proj_prompt_kernel(x_ref, sc_ref, sh_ref, g_ref, w_ref,
                        kt_ref, vt_ref, q16_ref, k16_ref, vt16_ref, km_ref, cb_ref, zb_ref, ga_ref, gb_ref, ba_ref,
                        *, hd, cols):
    hb = _prenorm(x_ref[...], g_ref[...], sc_ref[...], sh_ref[...]).astype(bf16)

    def proj(name):
        a, b = cols[name]
        return _dot(hb, w_ref[:, a:b])

    q, k, v = proj("q"), proj("k"), proj("v")
    tm = x_ref.shape[0]
    q16_ref[...] = (q * (hd ** -0.5 * LOG2E)).astype(bf16)
    km_ref[0] = jnp.mean(k, axis=0, keepdims=True)
    vrows = vt16_ref.shape[1]
    ones_row = (lax.broadcasted_iota(jnp.int32, (vrows - hd, tm), 0) == 0).astype(bf16)
    for g in range(k.shape[1] // LANES):
        cs = slice(g * LANES, (g + 1) * LANES)
        k16_ref[g] = k[:, cs].astype(bf16)
        kt_ref[cs, :] = k[:, cs].T
        vt = v[:, cs].T
        vt_ref[cs, :] = vt
        for e in range(LANES // hd):
            h = g * (LANES // hd) + e
            vt16_ref[h, 0:hd, :] = vt[e * hd:(e + 1) * hd].astype(bf16)
            vt16_ref[h, hd:vrows, :] = ones_row
    cb_ref[...] = proj("cb")
    zb_ref[...] = proj("zb")
    ga_ref[...] = proj("ga")
    gb_ref[...] = proj("gb")
    ba_ref[...] = proj("ba")


def _proj_sample_kernel(x_ref, sc_ref, sh_ref, g_ref, w_ref,
                        q_ref, k_ref, v_ref, cb_ref, zb_ref, ga_ref, gb_ref, ba_ref, *, cols):
    hb = _prenorm(x_ref[...], g_ref[...], sc_ref[...], sh_ref[...]).astype(bf16)
    for name, ref in (("q", q_ref), ("k", k_ref), ("v", v_ref), ("cb", cb_ref), ("zb", zb_ref),
                      ("ga", ga_ref), ("gb", gb_ref), ("ba", ba_ref)):
        a, b = cols[name]
        ref[...] = _dot(hb, w_ref[:, a:b])


def _layout_w_in(w_in, a_width, conv_ch, b_width, b_heads, d):
    o_q, o_k, o_v = 0, a_width, 2 * a_width
    o_cb = 3 * a_width
    o_zb = o_cb + conv_ch
    o_ba = o_zb + b_width
    o_ga = o_ba + 2 * b_heads
    o_gb = o_ga + d
    main = w_in[:, :o_ba]
    gates = w_in[:, o_ga:o_gb + d]
    ba = w_in[:, o_ba:o_ga]
    pad = jnp.zeros((w_in.shape[0], LANES - 2 * b_heads), w_in.dtype)
    w = jnp.concatenate([main, gates, ba, pad], axis=1).astype(bf16)
    cols = {"q": (o_q, o_k), "k": (o_k, o_v), "v": (o_v, o_cb), "cb": (o_cb, o_zb), "zb": (o_zb, o_ba),
            "ga": (o_ba, o_ba + d), "gb": (o_ba + d, o_ba + 2 * d), "ba": (o_ba + 2 * d, o_ba + 2 * d + LANES)}
    return w, cols


def _proj_prompt(x, sc, sh, g, w, cols, heads, hd, conv_ch, b_width):
    n, t, d = x.shape
    tm = MOBA_BLOCK
    nb = t // tm
    aw = heads * hd
    wcols = w.shape[1]
    row = lambda n_, i: (n_ * nb + i, 0)
    mod = lambda n_, i: (n_, 0, 0)
    tok_minor = lambda n_, i: (n_, 0, i)
    out_shape = (
        jax.ShapeDtypeStruct((n, aw, t), f32),
        jax.ShapeDtypeStruct((n, aw, t), f32),
        jax.ShapeDtypeStruct((n * t, aw), bf16),
        jax.ShapeDtypeStruct((n, nb, aw // LANES, tm, LANES), bf16),
        jax.ShapeDtypeStruct((n, nb, heads, hd + V_PAD_ROWS, tm), bf16),
        jax.ShapeDtypeStruct((n * nb, 1, aw), f32),
        jax.ShapeDtypeStruct((n * t, conv_ch), f32),
        jax.ShapeDtypeStruct((n * t, b_width), f32),
        jax.ShapeDtypeStruct((n * t, d), f32),
        jax.ShapeDtypeStruct((n * t, d), f32),
        jax.ShapeDtypeStruct((n * t, LANES), f32),
    )
    out_specs = (
        pl.BlockSpec((None, aw, tm), tok_minor),
        pl.BlockSpec((None, aw, tm), tok_minor),
        pl.BlockSpec((tm, aw), row),
        pl.BlockSpec((None, None, aw // LANES, tm, LANES), lambda n_, i: (n_, i, 0, 0, 0)),
        pl.BlockSpec((None, None, heads, hd + V_PAD_ROWS, tm), lambda n_, i: (n_, i, 0, 0, 0)),
        pl.BlockSpec((1, 1, aw), lambda n_, i: (n_ * nb + i, 0, 0)),
        pl.BlockSpec((tm, conv_ch), row),
        pl.BlockSpec((tm, b_width), row),
        pl.BlockSpec((tm, d), row),
        pl.BlockSpec((tm, d), row),
        pl.BlockSpec((tm, LANES), row),
    )
    return pl.pallas_call(
        functools.partial(_proj_prompt_kernel, hd=hd, cols=cols),
        grid=(n, nb),
        in_specs=[
            pl.BlockSpec((tm, d), row),
            pl.BlockSpec((None, 1, d), mod),
            pl.BlockSpec((None, 1, d), mod),
            pl.BlockSpec((1, d), lambda n_, i: (0, 0)),
            pl.BlockSpec((d, wcols), lambda n_, i: (0, 0)),
        ],
        out_specs=out_specs,
        out_shape=out_shape,
        compiler_params=_params(("arbitrary", "arbitrary")),
        name="proj_prompt",
    )(x.reshape(n * t, d), sc, sh, g, w)


def _proj_sample(x, sc, sh, g, w, cols):
    rows, d = x.shape
    wcols = w.shape[1]
    names = ("q", "k", "v", "cb", "zb", "ga", "gb", "ba")
    widths = [cols[nm][1] - cols[nm][0] for nm in names]
    full = lambda i: (0, 0)
    return pl.pallas_call(
        functools.partial(_proj_sample_kernel, cols=cols),
        grid=(1,),
        in_specs=[
            pl.BlockSpec((rows, d), full),
            pl.BlockSpec((rows, d), full),
            pl.BlockSpec((rows, d), full),
            pl.BlockSpec((1, d), full),
            pl.BlockSpec((d, wcols), full),
        ],
        out_specs=tuple(pl.BlockSpec((rows, wd), full) for wd in widths),
        out_shape=tuple(jax.ShapeDtypeStruct((rows, wd), f32) for wd in widths),
        compiler_params=_params(("arbitrary",)),
        name="proj_sample",
    )(x, sc, sh, g, w)


def _moba_prompt_kernel(q_ref, k_ref, vt_ref, kmh_ref, kml_ref, bias_ref, o_ref,
                        qq_ref, sel_ref, m_ref, acc_ref, s_ref, st_ref, *, heads, hd, topk):
    c = pl.program_id(1)
    qb = q_ref.shape[0]
    nbp = kmh_ref.shape[0]
    mb = k_ref.shape[2]
    per = LANES // hd
    pairs = heads // per
    own = (c * qb) // mb
    own_bias = 1 + (c * qb - own * mb) // qb
    lane = lax.broadcasted_iota(jnp.int32, (qb, LANES), 1)
    bid = lax.broadcasted_iota(jnp.int32, (nbp, qb), 0)

    for g in range(pairs):
        pair = slice(g * LANES, (g + 1) * LANES)
        qpair = q_ref[:, pair]
        for e in range(per):
            keep = jnp.logical_and(lane >= e * hd, lane < (e + 1) * hd)
            qq_ref[g, e * qb:(e + 1) * qb, :] = jnp.where(keep, qpair, jnp.zeros_like(qpair))
        qq = qq_ref[g]
        bs2 = _dot_nt(kmh_ref[:, pair], qq) + _dot_nt(kml_ref[:, pair], qq)
        for e in range(per):
            h = g * per + e
            bs = jnp.where(bid < own, bs2[:, e * qb:(e + 1) * qb], NEG_INF)
            sel = jnp.where(bid == own, 1.0, 0.0)
            for t in range(topk):
                mx = jnp.max(bs, axis=0, keepdims=True)
                idx = jnp.min(jnp.where(bs == mx, bid, nbp), axis=0, keepdims=True)
                hit = bid == idx
                sel = jnp.where(jnp.logical_and(hit, t < own), 1.0, sel)
                bs = jnp.where(hit, NEG_INF, bs)
            sel_ref[h] = sel
            m_ref[h] = jnp.full((1, qb), NEG_INF, f32)
            acc_ref[h] = jnp.zeros(acc_ref.shape[1:], f32)

    last = k_ref.shape[0] - 1

    def scores(i, slot):
        blk = jnp.where(i == 0, own, jnp.minimum(i - 1, last))
        tile = jnp.where(i == 0, own_bias, 0)
        live = i <= own
        for g in range(pairs):
            s2 = _dot_nt(k_ref[blk, g], qq_ref[g])
            for e in range(per):
                h = g * per + e
                on = jnp.logical_and(sel_ref[h, pl.ds(blk, 1), :] > 0.0, live)
                s = s2[:, e * qb:(e + 1) * qb] + bias_ref[tile, h]
                s_ref[slot, h] = s
                off = jnp.full((1, qb), blk - own, jnp.int32).astype(f32) * (_alibi_slope(h, heads) * mb * LOG2E)
                m = m_ref[h]
                m_new = jnp.where(on, jnp.maximum(m, jnp.max(s, axis=0, keepdims=True) + off), m)
                st_ref[slot, h] = jnp.concatenate([jnp.exp2(m - m_new), jnp.where(on, m_new - off, POS_INF)], axis=0)
                m_ref[h] = m_new

    def consume(i, slot):
        blk = jnp.where(i == 0, own, jnp.minimum(i - 1, last))
        for h in range(heads):
            st = st_ref[slot, h]
            p = jnp.exp2(s_ref[slot, h] - st[1:2])
            acc_ref[h] = st[0:1] * acc_ref[h] + _dot(vt_ref[blk, h], p.astype(bf16))

    depth = s_ref.shape[0] // 2
    for u in range(depth):
        scores(u, u)

    def body(i, carry):
        base = depth * lax.rem(i, 2)
        for u in range(depth):
            consume(depth * i + u, base + u)
        for u in range(depth):
            scores(depth * (i + 1) + u, depth - base + u)
        return carry

    trips = (own + depth) // depth
    lax.fori_loop(0, trips - 1, body, 0)
    final = trips - 1
    for u in range(depth):
        consume(depth * final + u, depth * lax.rem(final, 2) + u)
    for h in range(heads):
        acc = acc_ref[h]
        o_ref[h * hd:(h + 1) * hd, :] = (acc[:hd] / acc[hd:hd + 1]).astype(o_ref.dtype)


def _moba_prompt(q16, k16, vt16, km, n, t, heads, hd):
    aw = heads * hd
    nb = t // MOBA_BLOCK
    nq = t // Q_BLOCK
    per = LANES // hd
    pairs = heads // per
    vrows = vt16.shape[3]
    nbp = -(-nb // SUBLANES) * SUBLANES
    assert all(_alibi_slope(h, heads) * MOBA_BLOCK == int(_alibi_slope(h, heads) * MOBA_BLOCK) for h in range(heads))
    km = jnp.pad(km.reshape(n, nb, aw), ((0, 0), (0, nbp - nb), (0, 0)))
    km_hi = km.astype(bf16)
    km_lo = (km - km_hi.astype(f32)).astype(bf16)
    pos = jnp.arange(MOBA_BLOCK, dtype=f32)[None, :, None]
    slopes = jnp.asarray([_alibi_slope(h, heads) for h in range(heads)], f32)[:, None, None]
    plain = jnp.broadcast_to(slopes * pos * LOG2E, (heads, MOBA_BLOCK, Q_BLOCK))
    kpos = jnp.arange(MOBA_BLOCK)[:, None]
    qpos = jnp.arange(Q_BLOCK)[None, :]
    bias = jnp.stack([plain] + [jnp.where(kpos <= qpos + x * Q_BLOCK, plain, NEG_INF)
                                for x in range(MOBA_BLOCK // Q_BLOCK)])
    whole = lambda n_, c: (n_, 0, 0)
    resident = dict(pipeline_mode=pl.Buffered(1))
    return pl.pallas_call(
        functools.partial(_moba_prompt_kernel, heads=heads, hd=hd, topk=MOBA_TOPK),
        grid=(n, nq),
        in_specs=[
            pl.BlockSpec((None, Q_BLOCK, aw), lambda n_, c: (n_, c, 0)),
            pl.BlockSpec((None, nb, pairs, MOBA_BLOCK, LANES), lambda n_, c: (n_, 0, 0, 0, 0), **resident),
            pl.BlockSpec((None, nb, heads, vrows, MOBA_BLOCK), lambda n_, c: (n_, 0, 0, 0, 0), **resident),
            pl.BlockSpec((None, nbp, aw), whole),
            pl.BlockSpec((None, nbp, aw), whole),
            pl.BlockSpec((1 + MOBA_BLOCK // Q_BLOCK, heads, MOBA_BLOCK, Q_BLOCK), lambda n_, c: (0, 0, 0, 0)),
        ],
        out_specs=pl.BlockSpec((None, aw, Q_BLOCK), lambda n_, c: (n_, 0, c)),
        out_shape=jax.ShapeDtypeStruct((n, aw, t), bf16),
        scratch_shapes=[
            pltpu.VMEM((pairs, per * Q_BLOCK, LANES), bf16),
            pltpu.VMEM((heads, nbp, Q_BLOCK), f32),
            pltpu.VMEM((heads, 1, Q_BLOCK), f32),
            pltpu.VMEM((heads, vrows, Q_BLOCK), f32),
            pltpu.VMEM((2 * MOBA_DEPTH, heads, MOBA_BLOCK, Q_BLOCK), f32),
            pltpu.VMEM((2 * MOBA_DEPTH, heads, 2, Q_BLOCK), f32),
        ],
        compiler_params=_params(("arbitrary", "arbitrary")),
        name="moba_prompt",
    )(q16.reshape(n, t, aw), k16, vt16, km_hi, km_lo, bias)


def _sel_sample_kernel(pt_ref, q_ref, kn_ref, ck_ref, sel_ref, buf_ref, sem, *, layer, n_pages, ppb, topk, own):
    i = pl.program_id(0)
    slot = lax.rem(i, 2)

    def page_copy(sample, slot_, p):
        return pltpu.make_async_copy(ck_ref.at[layer, pt_ref[sample * n_pages + p]], buf_ref.at[slot_, p],
                                     sem.at[slot_])

    @pl.when(i == 0)
    def _():
        for p in range(n_pages):
            page_copy(0, 0, p).start()

    @pl.when(i + 1 < pl.num_programs(0))
    def _():
        for p in range(n_pages):
            page_copy(i + 1, 1 - slot, p).start()

    for p in range(n_pages):
        page_copy(i, slot, p).wait()

    heads, hd = q_ref.shape
    nbk = n_pages // ppb
    lane = lax.broadcasted_iota(jnp.int32, (hd, LANES), 1)
    q = q_ref[...]
    rows = []
    for h in range(heads):
        def add_block(b, m, h=h):
            s = buf_ref[slot, ppb * b, h]
            for j in range(1, ppb):
                s = s + buf_ref[slot, ppb * b + j, h]
            return jnp.where(lane == b, jnp.sum(s, axis=-1, keepdims=True), m)

        ksum_t = lax.fori_loop(0, nbk, add_block, jnp.zeros((hd, LANES), f32), unroll=True)
        rows.append(_dot_hi(q, ksum_t)[h:h + 1])
    lane8 = lax.broadcasted_iota(jnp.int32, (heads, LANES), 1)
    bs = jnp.concatenate(rows, axis=0) / MOBA_BLOCK
    sc_new = jnp.sum(kn_ref[...] * q, axis=-1, keepdims=True) / MOBA_BLOCK
    bs = jnp.where(lane8 == own, sc_new, bs)
    bs = jnp.where(lane8 < own, bs, NEG_INF)
    out = jnp.zeros((heads, LANES), jnp.int32)
    for t in range(topk):
        mx = jnp.max(bs, axis=-1, keepdims=True)
        idx = jnp.min(jnp.where(bs == mx, lane8, LANES), axis=-1, keepdims=True)
        out = jnp.where(lane8 == t, idx, out)
        bs = jnp.where(lane8 == idx, NEG_INF, bs)
    sel_ref[...] = out


def _attn_sample_kernel(pt_ref, sel_ref, q_ref, kn_ref, vn_ref, ck_ref, cv_ref, o_ref, kbuf, vbuf, sem,
                        *, layer, n_pages, ppb, topk, own, past):
    i = pl.program_id(0)
    slot = lax.rem(i, 2)
    heads, hd = q_ref.shape
    page = kbuf.shape[-1]

    def slab_copies(sample, slot_):
        out = []
        for h in range(heads):
            for t in range(topk):
                blk = sel_ref[(sample * heads + h) * topk + t]
                for j in range(ppb):
                    pg = pt_ref[sample * n_pages + ppb * blk + j]
                    r = (h * topk + t) * ppb + j
                    out.append(pltpu.make_async_copy(ck_ref.at[layer, pg, h], kbuf.at[slot_, r], sem.at[0, slot_]))
                    out.append(pltpu.make_async_copy(cv_ref.at[layer, pg, h], vbuf.at[slot_, r], sem.at[1, slot_]))
        return out

    @pl.when(i == 0)
    def _():
        for cp in slab_copies(0, 0):
            cp.start()

    @pl.when(i + 1 < pl.num_programs(0))
    def _():
        for cp in slab_copies(i + 1, 1 - slot):
            cp.start()

    for cp in slab_copies(i, slot):
        cp.wait()

    scale = hd ** -0.5
    q = q_ref[...]
    qb = (q * scale).astype(bf16)
    lane = lax.broadcasted_iota(jnp.int32, (1, page), 1)
    for h in range(heads):
        slope = _alibi_slope(h, heads)
        pieces = []
        for t in range(topk):
            blk = sel_ref[(i * heads + h) * topk + t]
            for j in range(ppb):
                r = (h * topk + t) * ppb + j
                if t < own:
                    s = _dot(qb, kbuf[slot, r].astype(bf16))[h:h + 1]
                    pos = blk * MOBA_BLOCK + j * page + lane
                    pieces.append(s - slope * (past - pos).astype(f32))
                else:
                    pieces.append(jnp.full((1, page), NEG_INF, f32))
        s_sel = jnp.concatenate(pieces, axis=1)
        s_own = jnp.sum(q[h:h + 1] * kn_ref[h:h + 1, :], axis=-1, keepdims=True) * scale
        m = jnp.maximum(jnp.max(s_sel, axis=-1, keepdims=True), s_own)
        p = jnp.exp(s_sel - m)
        p_own = jnp.exp(s_own - m)
        den = jnp.sum(p, axis=-1, keepdims=True) + p_own
        num = p_own * vn_ref[h:h + 1, :]
        for x in range(topk * ppb):
            px = jnp.broadcast_to(p[:, x * page:(x + 1) * page], (SUBLANES, page)).astype(bf16)
            num = num + _dot_nt(px, vbuf[slot, h * topk * ppb + x].astype(bf16))[0:1]
        o_ref[h:h + 1, :] = num / den


def _moba_sample(q_s, k_s, v_s, cache_k, cache_v, page_table, layer):
    dec = q_s.shape[0]
    _, n_pool, page, heads, hd = cache_k.shape
    n_pages = page_table.shape[1]
    past = n_pages * page
    ppb = MOBA_BLOCK // page
    assert MOBA_BLOCK % page == 0 and past % MOBA_BLOCK == 0 and page == LANES
    own = past // MOBA_BLOCK
    assert own < LANES
    topk = min(MOBA_TOPK, own + 1)
    q3 = q_s.reshape(dec, heads, hd)
    kn3 = k_s.reshape(dec, heads, hd)
    vn3 = v_s.reshape(dec, heads, hd)
    ck = jnp.transpose(cache_k, (0, 1, 3, 4, 2))
    cv = jnp.transpose(cache_v, (0, 1, 3, 4, 2))
    pt_flat = page_table.reshape(-1).astype(jnp.int32)
    per_sample = lambda i, *_: (i, 0, 0)
    hbm = pl.BlockSpec(memory_space=pl.ANY)

    sel = pl.pallas_call(
        functools.partial(_sel_sample_kernel, layer=layer, n_pages=n_pages, ppb=ppb, topk=topk, own=own),
        grid_spec=pltpu.PrefetchScalarGridSpec(
            num_scalar_prefetch=1,
            grid=(dec,),
            in_specs=[pl.BlockSpec((None, heads, hd), per_sample), pl.BlockSpec((None, heads, hd), per_sample), hbm],
            out_specs=pl.BlockSpec((None, heads, LANES), per_sample),
            scratch_shapes=[pltpu.VMEM((2, n_pages, heads, hd, page), f32), pltpu.SemaphoreType.DMA((2,))],
        ),
        out_shape=jax.ShapeDtypeStruct((dec, heads, LANES), jnp.int32),
        compiler_params=_params(("arbitrary",)),
        name="moba_sample_select",
    )(pt_flat, q3, kn3, ck)
    sel_flat = sel[:, :, :topk].reshape(-1)

    n_slabs = heads * topk * ppb
    o3 = pl.pallas_call(
        functools.partial(_attn_sample_kernel, layer=layer, n_pages=n_pages, ppb=ppb, topk=topk, own=own, past=past),
        grid_spec=pltpu.PrefetchScalarGridSpec(
            num_scalar_prefetch=2,
            grid=(dec,),
            in_specs=[pl.BlockSpec((None, heads, hd), per_sample), pl.BlockSpec((None, heads, hd), per_sample),
                      pl.BlockSpec((None, heads, hd), per_sample), hbm, hbm],
            out_specs=pl.BlockSpec((None, heads, hd), per_sample),
            scratch_shapes=[pltpu.VMEM((2, n_slabs, hd, page), f32), pltpu.VMEM((2, n_slabs, hd, page), f32),
                            pltpu.SemaphoreType.DMA((2, 2))],
        ),
        out_shape=jax.ShapeDtypeStruct((dec, heads, hd), f32),
        compiler_params=_params(("arbitrary",)),
        name="moba_sample_attend",
    )(pt_flat, sel_flat, q3, kn3, vn3, ck, cv)
    return o3.reshape(dec, heads * hd)


def _gdn_masks(td, ch):
    ri = lax.broadcasted_iota(jnp.int32, (td, td), 0)
    ci = lax.broadcasted_iota(jnp.int32, (td, td), 1)
    same = ri // ch == ci // ch
    masks = [jnp.logical_and(same, ri >= ci), jnp.logical_and(same, ri > ci)]
    b = 1
    while b < ch:
        masks.append(jnp.logical_and(jnp.logical_and(ri // (2 * b) == ci // (2 * b), ri % (2 * b) >= b),
                                     ci % (2 * b) < b))
        b *= 2
    return [m.astype(f32) for m in masks]


def _bdot(a, b):
    return jnp.einsum("bij,bjk->bik", a, b, preferred_element_type=f32)


def _bdot_nt(a, b):
    return jnp.einsum("bik,bjk->bij", a, b, preferred_element_type=f32)


def _bdot_tn(a, b):
    return jnp.einsum("bki,bkj->bij", a, b, preferred_element_type=f32)


def _bdot3(a, b):
    ah, al = _split(a)
    bh, bl = _split(b)
    return _bdot(ah, bh) + (_bdot(ah, bl) + _bdot(al, bh))


def _gdn_prompt_kernel(cb_ref, ba_ref, zb_ref, cw_ref, pa_ref, pb_ref, gn_ref, o_ref, sfin_ref,
                       xp_ref, s_ref, mk_ref, *, heads, dk, ch, taps):
    nseq, tt = cb_ref.shape[0], cb_ref.shape[1]
    td = mk_ref.shape[1]
    nt = tt // td
    bw = heads * dk
    pad = SUBLANES
    nlev = mk_ref.shape[0] - 2

    @pl.when(pl.program_id(0) == 0)
    def _():
        for x, m in enumerate(_gdn_masks(td, ch)):
            mk_ref[x] = m
        xp_ref[:, 0:pad, :] = jnp.zeros((nseq, pad, xp_ref.shape[2]), f32)
        s_ref[...] = jnp.zeros_like(s_ref)

    w = cw_ref[...]
    gn = gn_ref[...]
    chains = [(b, h, x) for b in range(nseq) for h in range(heads) for x in range(nt)]
    q_l, k_l, v_l, beta_l, g_l, z_l = [], [], [], [], [], []
    for b in range(nseq):
        xp_ref[b, pad:pad + tt, :] = cb_ref[b]
        y = w[0:1] * xp_ref[b, pad - (taps - 1):pad - (taps - 1) + tt, :]
        for j in range(1, taps):
            y = y + w[j:j + 1] * xp_ref[b, pad - (taps - 1) + j:pad - (taps - 1) + j + tt, :]
        xp_ref[b, 0:pad, :] = xp_ref[b, tt:tt + pad, :]
        c = _silu(y)
        ba = ba_ref[b]
        beta_all = jax.nn.sigmoid(ba)
        g_all = -jnp.exp(pa_ref[...]) * _softplus(ba + pb_ref[...])
        for h in range(heads):
            for x in range(nt):
                rows = slice(x * td, (x + 1) * td)
                q_l.append(c[rows, h * dk:(h + 1) * dk])
                k_l.append(c[rows, bw + h * dk:bw + (h + 1) * dk])
                v_l.append(c[rows, 2 * bw + h * dk:2 * bw + (h + 1) * dk])
                beta_l.append(jnp.broadcast_to(beta_all[rows, h:h + 1], (td, dk)))
                g_l.append(jnp.broadcast_to(g_all[rows, heads + h:heads + h + 1], (td, dk)))
                z_l.append(zb_ref[b, rows, h * dk:(h + 1) * dk])
    nc = len(chains)
    qn = _l2(jnp.stack(q_l)) * (dk ** -0.5)
    kn = _l2(jnp.stack(k_l))
    vh = jnp.stack(v_l)
    beta_b = jnp.stack(beta_l)
    incl, strict = mk_ref[0], mk_ref[1]
    gc2 = _dot_mask3(incl.astype(bf16), jnp.concatenate(g_l, axis=1))
    gcum = jnp.stack([gc2[:, x * dk:(x + 1) * dk] for x in range(nc)])
    gcol = jnp.concatenate([gcum] * (td // dk), axis=2) if td != dk else gcum
    diff = (gcol - jnp.swapaxes(gcol, 1, 2)) * incl
    decay = jnp.exp(diff) * incl
    eg = jnp.exp(gcum)
    kb = kn * beta_b
    kb16, kn16 = kb.astype(bf16), kn.astype(bf16)
    a = _bdot_nt(kb16, kn16) * decay * strict
    e = -(a * mk_ref[2])
    for lv in range(1, nlev):
        r = a * mk_ref[2 + lv]
        y2 = r + _bdot(e.astype(bf16), r.astype(bf16))
        e = e - y2 - _bdot(y2.astype(bf16), e.astype(bf16))
    rhs = jnp.concatenate([vh * beta_b, kb * eg], axis=2)
    uw = rhs + _bdot3(e, rhs)
    u, wv16 = uw[:, :, :dk], uw[:, :, dk:].astype(bf16)
    qk = _bdot_nt(qn.astype(bf16), kn16) * decay
    g_end = jnp.concatenate([jnp.broadcast_to(gcum[:, (x + 1) * ch - 1:(x + 1) * ch, :], (nc, ch, dk))
                             for x in range(td // ch)], axis=1)
    k_dec = (kn * jnp.exp(g_end - gcum)).astype(bf16)
    q_dec = (qn * eg).astype(bf16)
    g_last = jnp.exp(g_end)

    ns = nseq * heads
    pick = lambda arr, x: arr.reshape((ns, nt) + arr.shape[1:])[:, x]
    s = s_ref[...].reshape(ns, dk, dk)
    v_new = [[None] * (td // ch) for _ in range(nt)]
    o_int = [[None] * (td // ch) for _ in range(nt)]
    for x in range(nt):
        u_x, w_x, qd_x, kd_x, gl_x = pick(u, x), pick(wv16, x), pick(q_dec, x), pick(k_dec, x), pick(g_last, x)
        for cix in range(td // ch):
            sl = slice(cix * ch, (cix + 1) * ch)
            sb = s.astype(bf16)
            vn = u_x[:, sl] - _bdot(w_x[:, sl], sb)
            o_int[x][cix] = _bdot(qd_x[:, sl], sb)
            s = s * gl_x[:, (cix + 1) * ch - 1:(cix + 1) * ch, :] + _bdot_tn(kd_x[:, sl], vn.astype(bf16))
            v_new[x][cix] = vn
    s_ref[...] = s.reshape(nseq, heads, dk, dk)
    sfin_ref[...] = s.reshape(nseq, heads, dk, dk)
    vn_all = jnp.stack([jnp.concatenate(v_new[x], axis=1) for x in range(nt)], axis=1).reshape(nc, td, dk)
    oi_all = jnp.stack([jnp.concatenate(o_int[x], axis=1) for x in range(nt)], axis=1).reshape(nc, td, dk)
    o = oi_all + _bdot(qk.astype(bf16), vn_all.astype(bf16))
    on = (_rms(o) * gn * _silu(jnp.stack(z_l))).astype(o_ref.dtype)
    for i, (b, h, x) in enumerate(chains):
        o_ref[b, x * td:(x + 1) * td, h * dk:(h + 1) * dk] = on[i]


def _gdn_prompt(cb, ba, zb, conv_w, pa, pb, gn, n, t, heads, dk):
    td = dk
    tt = GDN_STEP_TOKENS if t % GDN_STEP_TOKENS == 0 else td
    conv_ch = cb.shape[1]
    bw = heads * dk
    nt = t // tt
    ch = GDN_CHUNK
    assert t % tt == 0 and tt % td == 0 and td % ch == 0 and ch & (ch - 1) == 0
    nmask = 2 + ch.bit_length() - 1
    tile = lambda i: (0, i, 0)
    const = lambda i: (0, 0)
    o, s_fin = pl.pallas_call(
        functools.partial(_gdn_prompt_kernel, heads=heads, dk=dk, ch=ch, taps=conv_w.shape[0]),
        grid=(nt,),
        in_specs=[
            pl.BlockSpec((n, tt, conv_ch), tile),
            pl.BlockSpec((n, tt, LANES), tile),
            pl.BlockSpec((n, tt, bw), tile),
            pl.BlockSpec(conv_w.shape, const),
            pl.BlockSpec((1, LANES), const),
            pl.BlockSpec((1, LANES), const),
            pl.BlockSpec((1, dk), const),
        ],
        out_specs=(
            pl.BlockSpec((n, tt, bw), tile),
            pl.BlockSpec((n, heads, dk, dk), lambda i: (0, 0, 0, 0)),
        ),
        out_shape=(
            jax.ShapeDtypeStruct((n, t, bw), bf16),
            jax.ShapeDtypeStruct((n, heads, dk, dk), f32),
        ),
        scratch_shapes=[pltpu.VMEM((n, tt + SUBLANES, conv_ch), f32), pltpu.VMEM((n, heads, dk, dk), f32),
                        pltpu.VMEM((nmask, td, td), f32)],
        compiler_params=_params(("arbitrary",)),
        name="gdn_prompt",
    )(cb.reshape(n, t, conv_ch), ba.reshape(n, t, LANES), zb.reshape(n, t, bw), conv_w, pa, pb, gn)
    return o.reshape(n * t, bw), s_fin


def _gdn_sample_kernel(cb_ref, b0_ref, b1_ref, b2_ref, ba_ref, zb_ref, s_ref, cw_ref, pa_ref, pb_ref, gn_ref,
                       o_ref, so_ref, *, heads, dk):
    sb = cb_ref.shape[0]
    bw = heads * dk
    w = cw_ref[...]
    y = w[0:1] * b0_ref[...]
    y = y + w[1:2] * b1_ref[...]
    y = y + w[2:3] * b2_ref[...]
    y = y + w[3:4] * cb_ref[...]
    c = _silu(y)
    ba = ba_ref[...]
    beta_all = jax.nn.sigmoid(ba)
    g_all = -jnp.exp(pa_ref[...]) * _softplus(ba + pb_ref[...])
    ri = lax.broadcasted_iota(jnp.int32, (dk, dk), 0)
    ci = lax.broadcasted_iota(jnp.int32, (dk, dk), 1)
    ones = jnp.ones((dk, dk), f32)
    gn = gn_ref[...]
    zrows = jnp.zeros((SUBLANES - 2, dk), f32)
    for h in range(heads):
        qn = _l2(c[:, h * dk:(h + 1) * dk]) * (dk ** -0.5)
        kn = _l2(c[:, bw + h * dk:bw + (h + 1) * dk])
        vh = c[:, 2 * bw + h * dk:2 * bw + (h + 1) * dk]
        beta_b = jnp.broadcast_to(beta_all[:, h:h + 1], (sb, dk))
        eg = jnp.exp(jnp.broadcast_to(g_all[:, heads + h:heads + h + 1], (sb, dk)))
        zh = zb_ref[:, h * dk:(h + 1) * dk]
        qk = jnp.sum(qn * kn, axis=-1, keepdims=True)
        for j in range(sb):
            r = slice(j, j + 1)
            s = s_ref[j, h]
            lhs = jnp.concatenate([kn[r] * beta_b[r] * eg[r], qn[r] * eg[r], zrows], axis=0)
            rs = _dot_hi(lhs, s)
            v_new = vh[r] * beta_b[r] - rs[0:1]
            o = rs[1:2] + qk[r] * v_new
            kcol = _dot_hi(jnp.where(ri == ci, jnp.broadcast_to(kn[r], (dk, dk)), 0.0), ones)
            so_ref[j, h] = s * eg[r] + kcol * v_new
            on = _rms(o) * gn * _silu(zh[r])
            o_ref[r, h * dk:(h + 1) * dk] = on.astype(o_ref.dtype)


def _gdn_sample(cb, bufs, ba, zb, state, conv_w, pa, pb, gn, heads, dk):
    dec, conv_ch = cb.shape
    bw = heads * dk
    sb = SUBLANES
    row = lambda i: (i, 0)
    const = lambda i: (0, 0)
    st = lambda i: (i, 0, 0, 0)
    return pl.pallas_call(
        functools.partial(_gdn_sample_kernel, heads=heads, dk=dk),
        grid=(dec // sb,),
        in_specs=[
            pl.BlockSpec((sb, conv_ch), row),
            pl.BlockSpec((sb, conv_ch), row),
            pl.BlockSpec((sb, conv_ch), row),
            pl.BlockSpec((sb, conv_ch), row),
            pl.BlockSpec((sb, LANES), row),
            pl.BlockSpec((sb, bw), row),
            pl.BlockSpec((sb, heads, dk, dk), st),
            pl.BlockSpec(conv_w.shape, const),
            pl.BlockSpec((1, LANES), const),
            pl.BlockSpec((1, LANES), const),
            pl.BlockSpec((1, dk), const),
        ],
        out_specs=(pl.BlockSpec((sb, bw), row), pl.BlockSpec((sb, heads, dk, dk), st)),
        out_shape=(jax.ShapeDtypeStruct((dec, bw), bf16), jax.ShapeDtypeStruct(state.shape, f32)),
        compiler_params=_params(("arbitrary",)),
        name="gdn_sample",
    )(cb, bufs[0], bufs[1], bufs[2], ba, zb, state, conv_w, pa, pb, gn)


def _merge_kernel(oa_ref, ob_ref, ga_ref, gb_ref, x_ref, gt_ref, gpm_ref, wa_ref, wb_ref, wo_ref, o_ref,
                  *, oa_transposed):
    if oa_transposed:
        ya = lax.dot_general(oa_ref[...], wa_ref[...], _TRANS_A, preferred_element_type=f32)
    else:
        ya = _dot(oa_ref[...].astype(bf16), wa_ref[...])
    yb = _dot(ob_ref[...], wb_ref[...])
    y = jax.nn.sigmoid(ga_ref[...]) * ya + jax.nn.sigmoid(gb_ref[...]) * yb
    y2 = _dot(y.astype(bf16), wo_ref[...])
    o_ref[...] = x_ref[...] + gt_ref[...] * (_rms(y2) * gpm_ref[...])


def _merge(oa, ob, ga, gb, x2, gt, gpm, wa, wb, wo, n, t, tm, oa_transposed):
    d = x2.shape[1]
    aw = wa.shape[0]
    bw = wb.shape[0]
    nt = t // tm
    row = lambda n_, i: (n_ * nt + i, 0)
    const = lambda n_, i: (0, 0)
    if oa_transposed:
        oa_spec = pl.BlockSpec((None, aw, tm), lambda n_, i: (n_, 0, i))
    else:
        oa_spec = pl.BlockSpec((tm, aw), row)
    return pl.pallas_call(
        functools.partial(_merge_kernel, oa_transposed=oa_transposed),
        grid=(n, nt),
        in_specs=[
            oa_spec,
            pl.BlockSpec((tm, bw), row),
            pl.BlockSpec((tm, d), row),
            pl.BlockSpec((tm, d), row),
            pl.BlockSpec((tm, d), row),
            pl.BlockSpec((None, gt.shape[1], d), lambda n_, i: (n_, 0, 0)),
            pl.BlockSpec((1, d), const),
            pl.BlockSpec(wa.shape, const),
            pl.BlockSpec(wb.shape, const),
            pl.BlockSpec(wo.shape, const),
        ],
        out_specs=pl.BlockSpec((tm, d), row),
        out_shape=jax.ShapeDtypeStruct(x2.shape, f32),
        compiler_params=_params(("arbitrary", "arbitrary")),
        name="merge",
    )(oa, ob, ga, gb, x2, gt, gpm, wa, wb, wo)


def _ffn_kernel(x_ref, sc_ref, sh_ref, gt_ref, gpre_ref, gpost_ref, wi_ref, wo_ref, o_ref, *, ff, fc):
    x1 = x_ref[...]
    hb = _prenorm(x1, gpre_ref[...], sc_ref[...], sh_ref[...]).astype(bf16)
    f = jnp.zeros(x1.shape, f32)
    for c in range(ff // fc):
        gate = _dot(hb, wi_ref[:, c * fc:(c + 1) * fc])
        up = _dot(hb, wi_ref[:, ff + c * fc:ff + (c + 1) * fc])
        f = f + _dot((_silu(gate) * up).astype(bf16), wo_ref[c * fc:(c + 1) * fc, :])
    o_ref[...] = x1 + gt_ref[...] * (_rms(f) * gpost_ref[...])


def _ffn(x2, sc, sh, gt, gpre, gpost, wi, wo, n, t, tm):
    d = x2.shape[1]
    ff = wo.shape[0]
    fc = ff // 2 if (ff // 2) % LANES == 0 else ff
    nt = t // tm
    row = lambda n_, i: (n_ * nt + i, 0)
    const = lambda n_, i: (0, 0)
    mod = pl.BlockSpec((None, sc.shape[1], d), lambda n_, i: (n_, 0, 0))
    return pl.pallas_call(
        functools.partial(_ffn_kernel, ff=ff, fc=fc),
        grid=(n, nt),
        in_specs=[
            pl.BlockSpec((tm, d), row), mod, mod, mod,
            pl.BlockSpec((1, d), const),
            pl.BlockSpec((1, d), const),
            pl.BlockSpec(wi.shape, const),
            pl.BlockSpec(wo.shape, const),
        ],
        out_specs=pl.BlockSpec((tm, d), row),
        out_shape=jax.ShapeDtypeStruct(x2.shape, f32),
        compiler_params=_params(("arbitrary", "arbitrary")),
        name="ffn",
    )(x2, sc, sh, gt, gpre, gpost, wi, wo)


def _fused_kernel(oa_ref, ob_ref, ga_ref, gb_ref, x_ref, gt1_ref, gpm_ref, wa_ref, wb_ref, wo_ref,
                  sc_ref, sh_ref, gt2_ref, gpre_ref, gpost_ref, wi_ref, wf_ref, o_ref, *, ff, fc):
    ya = lax.dot_general(oa_ref[...], wa_ref[...], _TRANS_A, preferred_element_type=f32)
    yb = _dot(ob_ref[...], wb_ref[...])
    y = jax.nn.sigmoid(ga_ref[...]) * ya + jax.nn.sigmoid(gb_ref[...]) * yb
    y2 = _dot(y.astype(bf16), wo_ref[...])
    x1 = x_ref[...] + gt1_ref[...] * (_rms(y2) * gpm_ref[...])
    hb = _prenorm(x1, gpre_ref[...], sc_ref[...], sh_ref[...]).astype(bf16)
    f = jnp.zeros(x1.shape, f32)
    for c in range(ff // fc):
        gate = _dot(hb, wi_ref[:, c * fc:(c + 1) * fc])
        up = _dot(hb, wi_ref[:, ff + c * fc:ff + (c + 1) * fc])
        f = f + _dot((_silu(gate) * up).astype(bf16), wf_ref[c * fc:(c + 1) * fc, :])
    o_ref[...] = x1 + gt2_ref[...] * (_rms(f) * gpost_ref[...])


def _fused(oa, ob, ga, gb, x2, gt1, gpm, wa, wb, wo, sc, sh, gt2, gpre, gpost, wi, wf, n, t, tm):
    d = x2.shape[1]
    aw, bw, ff = wa.shape[0], wb.shape[0], wf.shape[0]
    fc = ff // 2
    nt = t // tm
    row = lambda n_, i: (n_ * nt + i, 0)
    const = lambda n_, i: (0, 0)
    mod = pl.BlockSpec((None, 1, d), lambda n_, i: (n_, 0, 0))
    res = dict(pipeline_mode=pl.Buffered(1))
    wspec = lambda w: pl.BlockSpec(w.shape, const, **res)
    return pl.pallas_call(
        functools.partial(_fused_kernel, ff=ff, fc=fc),
        grid=(n, nt),
        in_specs=[pl.BlockSpec((None, aw, tm), lambda n_, i: (n_, 0, i)), pl.BlockSpec((tm, bw), row),
                  pl.BlockSpec((tm, d), row), pl.BlockSpec((tm, d), row), pl.BlockSpec((tm, d), row), mod,
                  pl.BlockSpec((1, d), const), wspec(wa), wspec(wb), wspec(wo), mod, mod, mod,
                  pl.BlockSpec((1, d), const), pl.BlockSpec((1, d), const), wspec(wi), wspec(wf)],
        out_specs=pl.BlockSpec((tm, d), row),
        out_shape=jax.ShapeDtypeStruct(x2.shape, f32),
        compiler_params=_params(("arbitrary", "arbitrary")),
        name="mlp_fused",
    )(oa, ob, ga, gb, x2, gt1, gpm, wa, wb, wo, sc, sh, gt2, gpre, gpost, wi, wf)


def kernel(x_prompt, x_sample, cache_k, cache_v, state_gdn, state_conv, page_table, c_prompt, c_sample,
           w_ada, b_ada, g_pre_mix, g_post_mix, g_pre_ffn, g_post_ffn, w_in, conv_w, a_log, dt_bias,
           gdn_norm_g, w_branch_a, w_branch_b, w_out, w_ffn_in, w_ffn_out):
    depth = w_in.shape[0]
    n, t, d = x_prompt.shape
    dec, dec_t, _ = x_sample.shape
    heads, hd = cache_k.shape[3], cache_k.shape[4]
    b_heads, dk = state_gdn.shape[2], state_gdn.shape[3]
    aw, bw = heads * hd, b_heads * dk
    conv_ch = conv_w.shape[2]
    taps = conv_w.shape[1]
    assert dec_t == 1 and heads == SUBLANES and 2 * hd == LANES and dk == LANES
    assert t % MOBA_BLOCK == 0 and MOBA_BLOCK % Q_BLOCK == 0 and dec % SUBLANES == 0 and taps == 4

    xp, xs = x_prompt, x_sample.reshape(dec, d)
    outs = {k: [] for k in ("kp", "vp", "ks", "vs", "sp", "ss", "bp", "bs")}
    rows = -(-(n + dec) // SUBLANES) * SUBLANES
    c_all = jnp.concatenate([c_prompt, c_sample, jnp.zeros((rows - n - dec, d), f32)], axis=0)

    for l in range(depth):
        mod = _ada(c_all, w_ada[l], b_ada[l])
        mods = [mod[:, j * d:(j + 1) * d] for j in range(N_MOD)]
        mp = [m[:n].reshape(n, 1, d) for m in mods]
        ms = [m[n:n + dec].reshape(1, dec, d) for m in mods]
        w, cols = _layout_w_in(w_in[l], aw, conv_ch, bw, b_heads, d)
        row = lambda a: a.reshape(1, -1)
        g1, g2, g3, g4 = row(g_pre_mix[l]), row(g_post_mix[l]), row(g_pre_ffn[l]), row(g_post_ffn[l])
        wa, wb, wo = w_branch_a[l].astype(bf16), w_branch_b[l].astype(bf16), w_out[l].astype(bf16)
        wi, wf = w_ffn_in[l].astype(bf16), w_ffn_out[l].astype(bf16)
        lane_pad = lambda a: jnp.zeros((1, LANES), f32).at[0, b_heads:2 * b_heads].set(a)
        pa, pb = lane_pad(a_log[l]), lane_pad(dt_bias[l])
        gn = row(gdn_norm_g[l])

        kt, vt, q16, k16, vt16, km, cb, zb, ga, gb, ba = _proj_prompt(xp, mp[1], mp[0], g1, w, cols, heads, hd,
                                                                      conv_ch, bw)
        oa_t = _moba_prompt(q16, k16, vt16, km, n, t, heads, hd)
        ob, s_fin = _gdn_prompt(cb, ba, zb, conv_w[l], pa, pb, gn, n, t, b_heads, dk)
        x2 = xp.reshape(n * t, d)
        tm = 2 * MOBA_BLOCK if t % (2 * MOBA_BLOCK) == 0 else MOBA_BLOCK
        xp = _fused(oa_t, ob, ga, gb, x2, mp[2], g2, wa, wb, wo, mp[4], mp[3], mp[5], g3, g4, wi, wf,
                    n, t, tm).reshape(n, t, d)
        outs["kp"].append(jnp.transpose(kt.reshape(n, heads, hd, t), (0, 3, 1, 2)))
        outs["vp"].append(jnp.transpose(vt.reshape(n, heads, hd, t), (0, 3, 1, 2)))
        outs["sp"].append(s_fin)
        outs["bp"].append(cb.reshape(n, t, conv_ch)[:, t - (taps - 1):])

        q_s, k_s, v_s, cb_s, zb_s, ga_s, gb_s, ba_s = _proj_sample(xs, ms[1][0], ms[0][0], g1, w, cols)
        oa_s = _moba_sample(q_s, k_s, v_s, cache_k, cache_v, page_table, l)
        bufs = [state_conv[l][:, j] for j in range(taps - 1)]
        ob_s, s_new = _gdn_sample(cb_s, bufs, ba_s, zb_s, state_gdn[l], conv_w[l], pa, pb, gn, b_heads, dk)
        x1s = _merge(oa_s, ob_s, ga_s, gb_s, xs, ms[2], g2, wa, wb, wo, 1, dec, dec, False)
        xs = _ffn(x1s, ms[4], ms[3], ms[5], g3, g4, wi, wf, 1, dec, dec)
        outs["ks"].append(k_s.reshape(dec, 1, heads, hd))
        outs["vs"].append(v_s.reshape(dec, 1, heads, hd))
        outs["ss"].append(s_new)
        outs["bs"].append(jnp.stack([bufs[1], bufs[2], cb_s], axis=1))

    st = lambda key: jnp.stack(outs[key])
    return (xp, xs.reshape(dec, 1, d), st("kp"), st("vp"), st("ks"), st("vs"), st("sp"), st("ss"), st("bp"), st("bs"))
```
